```python
import jax, jax.numpy as jnp
from jax import lax
import numpy as np

D_MODEL = 1024
BATCH = 8
SEQ = 8192
DEPTH = 4

CTX_LEN = 256
GRID_W = 64
NA_HEADS = 8
NA_HEAD_DIM = 64
NA_KH = 8
NA_KW = 16
M_HEADS = 4
M_HEAD_DIM = 128
M_CHUNK = 64
CONV_CH = 512
CONV_K = 31
N_EXPERTS = 16
EXPERT_FF = 2048
EC_FACTOR = 2
ROPE_BASE = 10000.0
NORM_EPS = 1e-6

NA_W = NA_HEADS * NA_HEAD_DIM
M_W = M_HEADS * M_HEAD_DIM
N_GATE_COLS = 4 * M_HEADS
IN_SPLITS = (NA_W, NA_W, NA_W, M_W, M_W, M_W, M_W, N_GATE_COLS, 2 * CONV_CH, 3 * D_MODEL)
IN_COLS = sum(IN_SPLITS)

kernel_name = 'hybrid_na_mlstm_conv_ec_moe_dit'


def rms_norm(x):
    xf = x.astype(jnp.float32)
    return (xf * lax.rsqrt(jnp.mean(xf * xf, -1, keepdims=True) + NORM_EPS)).astype(x.dtype)


def layer_norm(x, g, b):
    xf = x.astype(jnp.float32)
    mu = jnp.mean(xf, -1, keepdims=True)
    var = jnp.mean(jnp.square(xf - mu), -1, keepdims=True)
    return ((xf - mu) * lax.rsqrt(var + NORM_EPS)).astype(x.dtype) * g + b


def modulate(h, shift, scale):
    return h * (1 + scale) + shift


def split_heads(a, n):
    return a.reshape(*a.shape[:-1], n, a.shape[-1] // n)


def split_in(p):
    offs = [int(o) for o in np.cumsum(IN_SPLITS)[:-1]]
    return jnp.split(p, offs, axis=-1)


def axial_rope(x):
    T, d = x.shape[1], x.shape[-1]
    half = d // 2
    nf = half // 2
    t = jnp.arange(T)
    row = (t // GRID_W).astype(jnp.float32)
    col = (t % GRID_W).astype(jnp.float32)
    inv = ROPE_BASE ** (-jnp.arange(nf, dtype=jnp.float32) / nf)

    def rot(xh, pos):
        ang = pos[:, None] * inv[None, :]
        cos = jnp.cos(ang)[None, :, None, :].astype(x.dtype)
        sin = jnp.sin(ang)[None, :, None, :].astype(x.dtype)
        x1, x2 = xh[..., :nf], xh[..., nf:]
        return jnp.concatenate([x1 * cos - x2 * sin, x2 * cos + x1 * sin], -1)

    return jnp.concatenate([rot(x[..., :half], row), rot(x[..., half:], col)], -1)


def neighbourhood_attention(q, k, v, k_ctx, v_ctx, rpb):
    B, T, H, dh = q.shape
    rows = T // GRID_W
    kh = min(NA_KH, rows)
    scale = dh ** -0.5
    kg = k.reshape(B, rows, GRID_W, H, dh)
    vg = v.reshape(B, rows, GRID_W, H, dh)
    qg = jnp.moveaxis(q.reshape(B, rows, GRID_W, H, dh), 1, 0)
    row_start = jnp.clip(jnp.arange(rows) - kh // 2, 0, rows - kh)
    cq = jnp.arange(GRID_W)
    col_start = jnp.clip(cq - NA_KW // 2, 0, GRID_W - NA_KW)
    col_mask = (cq[None, :] >= col_start[:, None]) & (cq[None, :] < col_start[:, None] + NA_KW)
    col_idx = jnp.clip(cq[None, :] - cq[:, None] + NA_KW - 1, 0, 2 * NA_KW - 2)
    rpb_cols = rpb[:, :, col_idx]

    def row_block(args):
        r, q_r = args
        rs = row_start[r]
        k_r = lax.dynamic_slice_in_dim(kg, rs, kh, axis=1)
        v_r = lax.dynamic_slice_in_dim(vg, rs, kh, axis=1)
        row_idx = rs + jnp.arange(kh) - r + NA_KH - 1
        bias = jnp.transpose(rpb_cols[:, row_idx], (0, 2, 1, 3))
        s_lat = jnp.einsum('bqhd,bikhd->bhqik', q_r, k_r) * scale + bias
        s_lat = jnp.where(col_mask[:, None, :], s_lat, -jnp.inf).reshape(B, H, GRID_W, kh * GRID_W)
        s_ctx = jnp.einsum('bqhd,blhd->bhql', q_r, k_ctx) * scale
        p = jax.nn.softmax(jnp.concatenate([s_lat, s_ctx], -1).astype(jnp.float32), -1).astype(v.dtype)
        p_lat = p[..., :kh * GRID_W].reshape(B, H, GRID_W, kh, GRID_W)
        p_ctx = p[..., kh * GRID_W:]
        return (jnp.einsum('bhqik,bikhd->bqhd', p_lat, v_r)
                + jnp.einsum('bhql,blhd->bqhd', p_ctx, v_ctx))

    out = lax.map(row_block, (jnp.arange(rows), qg))
    return jnp.moveaxis(out, 0, 1).reshape(B, T, H * dh)


def context_attention(q, k, v):
    B, L, H, dh = q.shape
    s = jnp.einsum('bqhd,bkhd->bhqk', q, k) * dh ** -0.5
    p = jax.nn.softmax(s.astype(jnp.float32), -1).astype(v.dtype)
    return jnp.einsum('bhqk,bkhd->bqhd', p, v).reshape(B, L, H * dh)


def to_chunks(a):
    B, T, H = a.shape[:3]
    a = a.reshape(B, T // M_CHUNK, M_CHUNK, H, *a.shape[3:])
    return jnp.moveaxis(a, 3, 1)


def from_chunks(h):
    B, H, NC, L, d = h.shape
    return jnp.moveaxis(h, 1, 3).reshape(B, NC * L, H, d)


def mlstm_states(k, v, i_pre, logf, state0):
    b = jnp.cumsum(logf, -1)
    b_last = b[..., -1]
    a = b_last[..., None] - b + i_pre

    def step(carry, xs):
        C, n, m = carry
        k_c, v_c, a_c, bl = xs
        m_new = jnp.maximum(bl + m, jnp.max(a_c, -1))
        decay = jnp.exp(bl + m - m_new)
        w = jnp.exp(a_c - m_new[..., None])
        C_new = decay[..., None, None] * C + jnp.einsum('bhl,bhlv,bhlk->bhvk', w, v_c, k_c)
        n_new = decay[..., None] * n + jnp.einsum('bhl,bhlk->bhk', w, k_c)
        return (C_new, n_new, m_new), (C, n, m)

    xs = (jnp.moveaxis(k, 2, 0), jnp.moveaxis(v, 2, 0), jnp.moveaxis(a, 2, 0), jnp.moveaxis(b_last, 2, 0))
    final, starts = lax.scan(step, state0, xs)
    starts = tuple(jnp.moveaxis(s, 0, 2) for s in starts)
    return starts, final


def mlstm_outputs(q, k, v, i_pre, logf, starts):
    C0, n0, m0 = starts
    L = q.shape[-2]
    b = jnp.cumsum(logf, -1)
    lower_tri = jnp.tril(jnp.ones((L, L), dtype=bool))
    d_mat = jnp.where(lower_tri, b[..., :, None] - b[..., None, :] + i_pre[..., None, :], -jnp.inf)
    inter = b + m0[..., None]
    m = jnp.maximum(inter, jnp.max(d_mat, -1))
    s = jnp.einsum('bhcld,bhcsd->bhcls', q, k) * jnp.exp(d_mat - m[..., None])
    e_inter = jnp.exp(inter - m)
    num = (e_inter[..., None] * jnp.einsum('bhcvk,bhclk->bhclv', C0, q)
           + jnp.einsum('bhcls,bhcsv->bhclv', s, v))
    den = e_inter * jnp.einsum('bhck,bhclk->bhcl', n0, q) + jnp.sum(s, -1)
    return num / jnp.maximum(jnp.abs(den), jnp.exp(-m))[..., None]


def mlstm_direction(q_c, k_c, v_c, i_c, f_c, q, k, v, i, f, with_ctx_out):
    B, _, H, dk = k.shape
    dv = v.shape[-1]
    state0 = (jnp.zeros((B, H, dv, dk), jnp.float32), jnp.zeros((B, H, dk), jnp.float32),
              jnp.zeros((B, H), jnp.float32))
    qc, kc, vc, ic, lfc = to_chunks(q_c), to_chunks(k_c), to_chunks(v_c), to_chunks(i_c), to_chunks(jax.nn.log_sigmoid(f_c))
    ql, kl, vl, il, lfl = to_chunks(q), to_chunks(k), to_chunks(v), to_chunks(i), to_chunks(jax.nn.log_sigmoid(f))
    ctx_starts, ctx_final = mlstm_states(kc, vc, ic, lfc, state0)
    lat_starts, _ = mlstm_states(kl, vl, il, lfl, ctx_final)
    h_lat = from_chunks(mlstm_outputs(ql, kl, vl, il, lfl, lat_starts))
    h_ctx = from_chunks(mlstm_outputs(qc, kc, vc, ic, lfc, ctx_starts)) if with_ctx_out else None
    return h_lat, h_ctx


def mlstm_branch(q, k, v, o, g, q_c, k_c, v_c, o_c, g_c, gate_bias, norm_gain, with_ctx_out):
    dt = q.dtype
    f32 = lambda a: a.astype(jnp.float32)
    scale_k = M_HEAD_DIM ** -0.5
    lat = (f32(q), f32(k) * scale_k, f32(v))
    cx = (f32(q_c), f32(k_c) * scale_k, f32(v_c))
    gl = jnp.split(split_heads(f32(g) + f32(gate_bias), 4), 4, axis=-2)
    gl = [a[..., 0, :] for a in gl]
    gc = jnp.split(split_heads(f32(g_c) + f32(gate_bias), 4), 4, axis=-2)
    gc = [a[..., 0, :] for a in gc]
    flip = lambda a: jnp.flip(a, 1)
    h_fw, hc_fw = mlstm_direction(*cx, gc[0], gc[1], *lat, gl[0], gl[1], with_ctx_out)
    h_bw, hc_bw = mlstm_direction(*[flip(a) for a in cx], flip(gc[2]), flip(gc[3]),
                                  *[flip(a) for a in lat], flip(gl[2]), flip(gl[3]), with_ctx_out)

    def finish(h, og):
        B, T = h.shape[:2]
        h = rms_norm(h).reshape(B, T, M_W) * f32(norm_gain)
        return (h * jax.nn.sigmoid(f32(og))).astype(dt)

    y = finish(h_fw + flip(h_bw), o)
    y_c = finish(hc_fw + flip(hc_bw), o_c) if with_ctx_out else None
    return y, y_c


def conv_branch(u, w, b, ln_g, ln_b):
    a, gt = jnp.split(u, 2, axis=-1)
    h = a * jax.nn.sigmoid(gt)
    h = lax.conv_general_dilated(h, w[:, None, :], window_strides=(1,),
                                 padding=[(CONV_K // 2, CONV_K // 2)],
                                 dimension_numbers=('NWC', 'WIO', 'NWC'),
                                 feature_group_count=CONV_CH) + b
    return jax.nn.silu(layer_norm(h, ln_g, ln_b))


def merge(na, m, cv, gates, w_br_na, w_br_m, w_br_conv, w_out):
    g_na, g_m, g_cv = jnp.split(jax.nn.sigmoid(gates), 3, axis=-1)
    z = g_na * (na @ w_br_na) + g_m * (m @ w_br_m) + g_cv * (cv @ w_br_conv)
    return z @ w_out


def hybrid_mixer(h, h_ctx, w_in, na_q_gain, na_k_gain, na_rpb, m_gate_bias, m_norm_gain,
                 conv_w, conv_b, conv_ln_g, conv_ln_b, w_br_na, w_br_m, w_br_conv, w_out, with_ctx_out):
    l_naq, l_nak, l_nav, l_mq, l_mk, l_mv, l_mo, l_mg, l_cu, l_bg = split_in(h @ w_in)
    c_naq, c_nak, c_nav, c_mq, c_mk, c_mv, c_mo, c_mg, c_cu, c_bg = split_in(h_ctx @ w_in)
    k_c = rms_norm(split_heads(c_nak, NA_HEADS)) * na_k_gain
    v_c = split_heads(c_nav, NA_HEADS)
    q_l = rms_norm(split_heads(l_naq, NA_HEADS)) * na_q_gain
    k_l = rms_norm(split_heads(l_nak, NA_HEADS)) * na_k_gain
    na_l = neighbourhood_attention(q_l, k_l, split_heads(l_nav, NA_HEADS), k_c, v_c, na_rpb)
    m_l, m_c = mlstm_branch(axial_rope(split_heads(l_mq, M_HEADS)), axial_rope(split_heads(l_mk, M_HEADS)),
                            split_heads(l_mv, M_HEADS), l_mo, l_mg,
                            split_heads(c_mq, M_HEADS), split_heads(c_mk, M_HEADS),
                            split_heads(c_mv, M_HEADS), c_mo, c_mg, m_gate_bias, m_norm_gain, with_ctx_out)
    cv_l = conv_branch(l_cu, conv_w, conv_b, conv_ln_g, conv_ln_b)
    y_l = merge(na_l, m_l, cv_l, l_bg, w_br_na, w_br_m, w_br_conv, w_out)
    if not with_ctx_out:
        return y_l, None
    q_c = rms_norm(split_heads(c_naq, NA_HEADS)) * na_q_gain
    na_c = context_attention(q_c, k_c, v_c)
    cv_c = conv_branch(c_cu, conv_w, conv_b, conv_ln_g, conv_ln_b)
    y_c = merge(na_c, m_c, cv_c, c_bg, w_br_na, w_br_m, w_br_conv, w_out)
    return y_l, y_c


def expert_choice_ffn(h, w_router, w_gate, w_up, w_down):
    B, T, D = h.shape
    cap = EC_FACTOR * T // N_EXPERTS
    aff = jax.nn.softmax((h @ w_router).astype(jnp.float32), -1)
    g, idx = lax.top_k(jnp.swapaxes(aff, 1, 2), cap)
    xin = jax.vmap(lambda hb, ib: hb[ib])(h, idx)
    a = jnp.einsum('becd,edf->becf', xin, w_gate)
    u = jnp.einsum('becd,edf->becf', xin, w_up)
    y = jnp.einsum('becf,efd->becd', jax.nn.silu(a) * u, w_down) * g[..., None].astype(h.dtype)
    return jax.vmap(lambda yb, ib: jnp.zeros((T, D), h.dtype).at[ib.reshape(-1)].add(yb.reshape(-1, D)))(y, idx)


def setup_inputs(seed: int = 0) -> dict:
    key = jax.random.key(seed)
    ks = jax.random.split(key, 26)
    nrm = lambda k, shape, s: jax.random.normal(k, shape, jnp.float32) * s
    D = D_MODEL
    i_bias = nrm(ks[10], (DEPTH, 2, M_HEADS), 0.1)
    f_bias = jnp.linspace(3.0, 6.0, M_HEADS, dtype=jnp.float32) + nrm(ks[11], (DEPTH, 2, M_HEADS), 0.1)
    m_gate_bias = jnp.stack([i_bias[:, 0], f_bias[:, 0], i_bias[:, 1], f_bias[:, 1]], axis=1).reshape(DEPTH, N_GATE_COLS)
    return {
        'x': nrm(ks[0], (BATCH, SEQ, D), 1.0),
        'c': nrm(ks[1], (BATCH, D), 1.0),
        'ctx': nrm(ks[2], (BATCH, CTX_LEN, D), 1.0),
        'c_ctx': nrm(ks[3], (D,), 1.0),
        'w_mod': nrm(ks[4], (DEPTH, D, 6 * D), 0.5 * D ** -0.5),
        'b_mod': nrm(ks[5], (DEPTH, 6 * D), 0.02),
        'w_in': nrm(ks[6], (DEPTH, D, IN_COLS), D ** -0.5),
        'na_q_gain': 1.0 + nrm(ks[7], (DEPTH, NA_HEAD_DIM), 0.02),
        'na_k_gain': 1.0 + nrm(ks[8], (DEPTH, NA_HEAD_DIM), 0.02),
        'na_rpb': nrm(ks[9], (DEPTH, NA_HEADS, 2 * NA_KH - 1, 2 * NA_KW - 1), 0.1),
        'm_gate_bias': m_gate_bias,
        'm_norm_gain': 1.0 + nrm(ks[12], (DEPTH, M_W), 0.02),
        'conv_w': nrm(ks[13], (DEPTH, CONV_K, CONV_CH), CONV_K ** -0.5),
        'conv_b': nrm(ks[14], (DEPTH, CONV_CH), 0.02),
        'conv_ln_g': 1.0 + nrm(ks[15], (DEPTH, CONV_CH), 0.02),
        'conv_ln_b': nrm(ks[16], (DEPTH, CONV_CH), 0.02),
        'w_br_na': nrm(ks[17], (DEPTH, NA_W, D), NA_W ** -0.5),
        'w_br_m': nrm(ks[18], (DEPTH, M_W, D), M_W ** -0.5),
        'w_br_conv': nrm(ks[19], (DEPTH, CONV_CH, D), CONV_CH ** -0.5),
        'w_out': nrm(ks[20], (DEPTH, D, D), D ** -0.5),
        'w_router': nrm(ks[21], (DEPTH, D, N_EXPERTS), D ** -0.5),
        'w_gate': nrm(ks[22], (DEPTH, N_EXPERTS, D, EXPERT_FF), D ** -0.5),
        'w_up': nrm(ks[23], (DEPTH, N_EXPERTS, D, EXPERT_FF), D ** -0.5),
        'w_down': nrm(ks[24], (DEPTH, N_EXPERTS, EXPERT_FF, D), EXPERT_FF ** -0.5),
    }


def reference(x, c, ctx, c_ctx, w_mod, b_mod, w_in, na_q_gain, na_k_gain, na_rpb, m_gate_bias,
              m_norm_gain, conv_w, conv_b, conv_ln_g, conv_ln_b, w_br_na, w_br_m, w_br_conv, w_out,
              w_router, w_gate, w_up, w_down):
    for l in range(DEPTH):
        keep_ctx = l < DEPTH - 1
        sh1, sc1, g1, sh2, sc2, g2 = jnp.split((jax.nn.silu(c) @ w_mod[l] + b_mod[l])[:, None, :], 6, axis=-1)
        csh1, csc1, cg1, csh2, csc2, cg2 = jnp.split(jax.nn.silu(c_ctx) @ w_mod[l] + b_mod[l], 6, axis=-1)
        y, y_ctx = hybrid_mixer(modulate(rms_norm(x), sh1, sc1), modulate(rms_norm(ctx), csh1, csc1),
                                w_in[l], na_q_gain[l], na_k_gain[l], na_rpb[l], m_gate_bias[l], m_norm_gain[l],
                                conv_w[l], conv_b[l], conv_ln_g[l], conv_ln_b[l],
                                w_br_na[l], w_br_m[l], w_br_conv[l], w_out[l], keep_ctx)
        x = x + g1 * y
        x = x + g2 * expert_choice_ffn(modulate(rms_norm(x), sh2, sc2), w_router[l], w_gate[l], w_up[l], w_down[l])
        if keep_ctx:
            ctx = ctx + cg1 * y_ctx
            ctx = ctx + cg2 * expert_choice_ffn(modulate(rms_norm(ctx), csh2, csc2), w_router[l], w_gate[l], w_up[l], w_down[l])
    return x
```

```python
import functools

import jax, jax.numpy as jnp
from jax import lax
import numpy as np
from jax.experimental import pallas as pl
from jax.experimental.pallas import tpu as pltpu

D_MODEL = 1024
DEPTH = 4
CTX_LEN = 256
GRID_W = 64
NA_HEADS = 8
NA_HEAD_DIM = 64
NA_KH = 8
NA_KW = 16
M_HEADS = 4
M_HEAD_DIM = 128
M_CHUNK = 64
CONV_CH = 512
CONV_K = 31
N_EXPERTS = 16
EXPERT_FF = 2048
EC_FACTOR = 2
ROPE_BASE = 10000.0
NORM_EPS = 1e-6

NA_W = NA_HEADS * NA_HEAD_DIM
M_W = M_HEADS * M_HEAD_DIM
N_GATE_COLS = 4 * M_HEADS
IN_SPLITS = (NA_W, NA_W, NA_W, M_W, M_W, M_W, M_W, N_GATE_COLS, 2 * CONV_CH, 3 * D_MODEL)

VMEM_LIMIT_BYTES = 56 * 1024 * 1024


def rms_norm(x):
    xf = x.astype(jnp.float32)
    return (xf * lax.rsqrt(jnp.mean(xf * xf, -1, keepdims=True) + NORM_EPS)).astype(x.dtype)


def layer_norm(x, g, b):
    xf = x.astype(jnp.float32)
    mu = jnp.mean(xf, -1, keepdims=True)
    var = jnp.mean(jnp.square(xf - mu), -1, keepdims=True)
    return ((xf - mu) * lax.rsqrt(var + NORM_EPS)).astype(x.dtype) * g + b


def modulate(h, shift, scale):
    return h * (1 + scale) + shift


def split_heads(a, n):
    return a.reshape(*a.shape[:-1], n, a.shape[-1] // n)


def split_in(p):
    offs = [int(o) for o in np.cumsum(IN_SPLITS)[:-1]]
    return jnp.split(p, offs, axis=-1)


def axial_rope(x):
    T, d = x.shape[1], x.shape[-1]
    half = d // 2
    nf = half // 2
    t = jnp.arange(T)
    row = (t // GRID_W).astype(jnp.float32)
    col = (t % GRID_W).astype(jnp.float32)
    inv = ROPE_BASE ** (-jnp.arange(nf, dtype=jnp.float32) / nf)

    def rot(xh, pos):
        ang = pos[:, None] * inv[None, :]
        cos = jnp.cos(ang)[None, :, None, :].astype(x.dtype)
        sin = jnp.sin(ang)[None, :, None, :].astype(x.dtype)
        x1, x2 = xh[..., :nf], xh[..., nf:]
        return jnp.concatenate([x1 * cos - x2 * sin, x2 * cos + x1 * sin], -1)

    return jnp.concatenate([rot(x[..., :half], row), rot(x[..., half:], col)], -1)


def neighbourhood_attention(q, k, v, k_ctx, v_ctx, rpb):
    B, T, H, dh = q.shape
    rows = T // GRID_W
    kh = min(NA_KH, rows)
    scale = dh ** -0.5
    kg = k.reshape(B, rows, GRID_W, H, dh)
    vg = v.reshape(B, rows, GRID_W, H, dh)
    qg = jnp.moveaxis(q.reshape(B, rows, GRID_W, H, dh), 1, 0)
    row_start = jnp.clip(jnp.arange(rows) - kh // 2, 0, rows - kh)
    cq = jnp.arange(GRID_W)
    col_start = jnp.clip(cq - NA_KW // 2, 0, GRID_W - NA_KW)
    col_mask = (cq[None, :] >= col_start[:, None]) & (cq[None, :] < col_start[:, None] + NA_KW)
    col_idx = jnp.clip(cq[None, :] - cq[:, None] + NA_KW - 1, 0, 2 * NA_KW - 2)
    rpb_cols = rpb[:, :, col_idx]

    def row_block(args):
        r, q_r = args
        rs = row_start[r]
        k_r = lax.dynamic_slice_in_dim(kg, rs, kh, axis=1)
        v_r = lax.dynamic_slice_in_dim(vg, rs, kh, axis=1)
        row_idx = rs + jnp.arange(kh) - r + NA_KH - 1
        bias = jnp.transpose(rpb_cols[:, row_idx], (0, 2, 1, 3))
        s_lat = jnp.einsum('bqhd,bikhd->bhqik', q_r, k_r) * scale + bias
        s_lat = jnp.where(col_mask[:, None, :], s_lat, -jnp.inf).reshape(B, H, GRID_W, kh * GRID_W)
        s_ctx = jnp.einsum('bqhd,blhd->bhql', q_r, k_ctx) * scale
        p = jax.nn.softmax(jnp.concatenate([s_lat, s_ctx], -1).astype(jnp.float32), -1).astype(v.dtype)
        p_lat = p[..., :kh * GRID_W].reshape(B, H, GRID_W, kh, GRID_W)
        p_ctx = p[..., kh * GRID_W:]
        return (jnp.einsum('bhqik,bikhd->bqhd', p_lat, v_r)
                + jnp.einsum('bhql,blhd->bqhd', p_ctx, v_ctx))

    out = lax.map(row_block, (jnp.arange(rows), qg))
    return jnp.moveaxis(out, 0, 1).reshape(B, T, H * dh)


def context_attention(q, k, v):
    B, L, H, dh = q.shape
    s = jnp.einsum('bqhd,bkhd->bhqk', q, k) * dh ** -0.5
    p = jax.nn.softmax(s.astype(jnp.float32), -1).astype(v.dtype)
    return jnp.einsum('bhqk,bkhd->bqhd', p, v).reshape(B, L, H * dh)


def to_chunks(a):
    B, T, H = a.shape[:3]
    a = a.reshape(B, T // M_CHUNK, M_CHUNK, H, *a.shape[3:])
    return jnp.moveaxis(a, 3, 1)


def from_chunks(h):
    B, H, NC, L, d = h.shape
    return jnp.moveaxis(h, 1, 3).reshape(B, NC * L, H, d)


def mlstm_states(k, v, i_pre, logf, state0):
    b = jnp.cumsum(logf, -1)
    b_last = b[..., -1]
    a = b_last[..., None] - b + i_pre

    def step(carry, xs):
        C, n, m = carry
        k_c, v_c, a_c, bl = xs
        m_new = jnp.maximum(bl + m, jnp.max(a_c, -1))
        decay = jnp.exp(bl + m - m_new)
        w = jnp.exp(a_c - m_new[..., None])
        C_new = decay[..., None, None] * C + jnp.einsum('bhl,bhlv,bhlk->bhvk', w, v_c, k_c)
        n_new = decay[..., None] * n + jnp.einsum('bhl,bhlk->bhk', w, k_c)
        return (C_new, n_new, m_new), (C, n, m)

    xs = (jnp.moveaxis(k, 2, 0), jnp.moveaxis(v, 2, 0), jnp.moveaxis(a, 2, 0), jnp.moveaxis(b_last, 2, 0))
    final, starts = lax.scan(step, state0, xs)
    starts = tuple(jnp.moveaxis(s, 0, 2) for s in starts)
    return starts, final


def mlstm_outputs(q, k, v, i_pre, logf, starts):
    C0, n0, m0 = starts
    L = q.shape[-2]
    b = jnp.cumsum(logf, -1)
    lower_tri = jnp.tril(jnp.ones((L, L), dtype=bool))
    d_mat = jnp.where(lower_tri, b[..., :, None] - b[..., None, :] + i_pre[..., None, :], -jnp.inf)
    inter = b + m0[..., None]
    m = jnp.maximum(inter, jnp.max(d_mat, -1))
    s = jnp.einsum('bhcld,bhcsd->bhcls', q, k) * jnp.exp(d_mat - m[..., None])
    e_inter = jnp.exp(inter - m)
    num = (e_inter[..., None] * jnp.einsum('bhcvk,bhclk->bhclv', C0, q)
           + jnp.einsum('bhcls,bhcsv->bhclv', s, v))
    den = e_inter * jnp.einsum('bhck,bhclk->bhcl', n0, q) + jnp.sum(s, -1)
    return num / jnp.maximum(jnp.abs(den), jnp.exp(-m))[..., None]


def mlstm_direction(q_c, k_c, v_c, i_c, f_c, q, k, v, i, f, with_ctx_out):
    B, _, H, dk = k.shape
    dv = v.shape[-1]
    state0 = (jnp.zeros((B, H, dv, dk), jnp.float32), jnp.zeros((B, H, dk), jnp.float32),
              jnp.zeros((B, H), jnp.float32))
    qc, kc, vc, ic, lfc = to_chunks(q_c), to_chunks(k_c), to_chunks(v_c), to_chunks(i_c), to_chunks(jax.nn.log_sigmoid(f_c))
    ql, kl, vl, il, lfl = to_chunks(q), to_chunks(k), to_chunks(v), to_chunks(i), to_chunks(jax.nn.log_sigmoid(f))
    ctx_starts, ctx_final = mlstm_states(kc, vc, ic, lfc, state0)
    lat_starts, _ = mlstm_states(kl, vl, il, lfl, ctx_final)
    h_lat = from_chunks(mlstm_outputs(ql, kl, vl, il, lfl, lat_starts))
    h_ctx = from_chunks(mlstm_outputs(qc, kc, vc, ic, lfc, ctx_starts)) if with_ctx_out else None
    return h_lat, h_ctx


def mlstm_branch(q, k, v, o, g, q_c, k_c, v_c, o_c, g_c, gate_bias, norm_gain, with_ctx_out):
    dt = q.dtype
    f32 = lambda a: a.astype(jnp.float32)
    scale_k = M_HEAD_DIM ** -0.5
    lat = (f32(q), f32(k) * scale_k, f32(v))
    cx = (f32(q_c), f32(k_c) * scale_k, f32(v_c))
    gl = jnp.split(split_heads(f32(g) + f32(gate_bias), 4), 4, axis=-2)
    gl = [a[..., 0, :] for a in gl]
    gc = jnp.split(split_heads(f32(g_c) + f32(gate_bias), 4), 4, axis=-2)
    gc = [a[..., 0, :] for a in gc]
    flip = lambda a: jnp.flip(a, 1)
    h_fw, hc_fw = mlstm_direction(*cx, gc[0], gc[1], *lat, gl[0], gl[1], with_ctx_out)
    h_bw, hc_bw = mlstm_direction(*[flip(a) for a in cx], flip(gc[2]), flip(gc[3]),
                                  *[flip(a) for a in lat], flip(gl[2]), flip(gl[3]), with_ctx_out)

    def finish(h, og):
        B, T = h.shape[:2]
        h = rms_norm(h).reshape(B, T, M_W) * f32(norm_gain)
        return (h * jax.nn.sigmoid(f32(og))).astype(dt)

    y = finish(h_fw + flip(h_bw), o)
    y_c = finish(hc_fw + flip(hc_bw), o_c) if with_ctx_out else None
    return y, y_c


def conv_branch(u, w, b, ln_g, ln_b):
    a, gt = jnp.split(u, 2, axis=-1)
    h = a * jax.nn.sigmoid(gt)
    h = lax.conv_general_dilated(h, w[:, None, :], window_strides=(1,),
                                 padding=[(CONV_K // 2, CONV_K // 2)],
                                 dimension_numbers=('NWC', 'WIO', 'NWC'),
                                 feature_group_count=CONV_CH) + b
    return jax.nn.silu(layer_norm(h, ln_g, ln_b))


def merge(na, m, cv, gates, w_br_na, w_br_m, w_br_conv, w_out):
    g_na, g_m, g_cv = jnp.split(jax.nn.sigmoid(gates), 3, axis=-1)
    z = g_na * (na @ w_br_na) + g_m * (m @ w_br_m) + g_cv * (cv @ w_br_conv)
    return z @ w_out


def hybrid_mixer(h, h_ctx, w_in, na_q_gain, na_k_gain, na_rpb, m_gate_bias, m_norm_gain,
                 conv_w, conv_b, conv_ln_g, conv_ln_b, w_br_na, w_br_m, w_br_conv, w_out, with_ctx_out):
    l_naq, l_nak, l_nav, l_mq, l_mk, l_mv, l_mo, l_mg, l_cu, l_bg = split_in(h @ w_in)
    c_naq, c_nak, c_nav, c_mq, c_mk, c_mv, c_mo, c_mg, c_cu, c_bg = split_in(h_ctx @ w_in)
    k_c = rms_norm(split_heads(c_nak, NA_HEADS)) * na_k_gain
    v_c = split_heads(c_nav, NA_HEADS)
    q_l = rms_norm(split_heads(l_naq, NA_HEADS)) * na_q_gain
    k_l = rms_norm(split_heads(l_nak, NA_HEADS)) * na_k_gain
    na_l = neighbourhood_attention(q_l, k_l, split_heads(l_nav, NA_HEADS), k_c, v_c, na_rpb)
    m_l, m_c = mlstm_branch(axial_rope(split_heads(l_mq, M_HEADS)), axial_rope(split_heads(l_mk, M_HEADS)),
                            split_heads(l_mv, M_HEADS), l_mo, l_mg,
                            split_heads(c_mq, M_HEADS), split_heads(c_mk, M_HEADS),
                            split_heads(c_mv, M_HEADS), c_mo, c_mg, m_gate_bias, m_norm_gain, with_ctx_out)
    cv_l = conv_branch(l_cu, conv_w, conv_b, conv_ln_g, conv_ln_b)
    y_l = merge(na_l, m_l, cv_l, l_bg, w_br_na, w_br_m, w_br_conv, w_out)
    if not with_ctx_out:
        return y_l, None
    q_c = rms_norm(split_heads(c_naq, NA_HEADS)) * na_q_gain
    na_c = context_attention(q_c, k_c, v_c)
    cv_c = conv_branch(c_cu, conv_w, conv_b, conv_ln_g, conv_ln_b)
    y_c = merge(na_c, m_c, cv_c, c_bg, w_br_na, w_br_m, w_br_conv, w_out)
    return y_l, y_c


FFN_FF_TILE = 512


def _ffn_kernel(x_ref, wg_ref, wu_ref, wd_ref, o_ref, acc_ref):
    f = pl.program_id(2)

    @pl.when(f == 0)
    def _():
        acc_ref[...] = jnp.zeros_like(acc_ref)

    x = x_ref[0, 0]
    a = jnp.dot(x, wg_ref[0], preferred_element_type=jnp.float32)
    u = jnp.dot(x, wu_ref[0], preferred_element_type=jnp.float32)
    hmid = (a * jax.nn.sigmoid(a) * u).astype(jnp.bfloat16)
    acc_ref[...] += jnp.dot(hmid, wd_ref[0], preferred_element_type=jnp.float32)

    @pl.when(f == pl.num_programs(2) - 1)
    def _():
        o_ref[0, 0] = acc_ref[...]


def expert_ffn(xin, w_gate, w_up, w_down):
    B, E, cap, D = xin.shape
    F = w_gate.shape[-1]
    tf = min(FFN_FF_TILE, F)
    return pl.pallas_call(
        _ffn_kernel,
        grid=(B, E, F // tf),
        in_specs=[
            pl.BlockSpec((1, 1, cap, D), lambda b, e, f: (b, e, 0, 0)),
            pl.BlockSpec((1, D, tf), lambda b, e, f: (e, 0, f)),
            pl.BlockSpec((1, D, tf), lambda b, e, f: (e, 0, f)),
            pl.BlockSpec((1, tf, D), lambda b, e, f: (e, f, 0)),
        ],
        out_specs=pl.BlockSpec((1, 1, cap, D), lambda b, e, f: (b, e, 0, 0)),
        out_shape=jax.ShapeDtypeStruct((B, E, cap, D), jnp.float32),
        scratch_shapes=[pltpu.VMEM((cap, D), jnp.float32)],
        compiler_params=pltpu.CompilerParams(
            dimension_semantics=("arbitrary", "arbitrary", "arbitrary"),
            vmem_limit_bytes=VMEM_LIMIT_BYTES),
        name="expert_ffn",
    )(xin, w_gate, w_up, w_down)


def expert_choice_ffn(h, w_router, w_gate, w_up, w_down):
    B, T, D = h.shape
    cap = EC_FACTOR * T // N_EXPERTS
    aff = jax.nn.softmax((h @ w_router).astype(jnp.float32), -1)
    g, idx = lax.top_k(jnp.swapaxes(aff, 1, 2), cap)
    xin = jax.vmap(lambda hb, ib: hb[ib])(h, idx)
    y = expert_ffn(xin.astype(jnp.bfloat16), w_gate, w_up, w_down) * g[..., None].astype(h.dtype)
    return jax.vmap(lambda yb, ib: jnp.zeros((T, D), h.dtype).at[ib.reshape(-1)].add(yb.reshape(-1, D)))(y, idx)


def kernel(x, c, ctx, c_ctx, w_mod, b_mod, w_in, na_q_gain, na_k_gain, na_rpb, m_gate_bias,
           m_norm_gain, conv_w, conv_b, conv_ln_g, conv_ln_b, w_br_na, w_br_m, w_br_conv, w_out,
           w_router, w_gate, w_up, w_down):
    bf = jnp.bfloat16
    for l in range(DEPTH):
        keep_ctx = l < DEPTH - 1
        wg, wu, wd = w_gate[l].astype(bf), w_up[l].astype(bf), w_down[l].astype(bf)
        sh1, sc1, g1, sh2, sc2, g2 = jnp.split((jax.nn.silu(c) @ w_mod[l] + b_mod[l])[:, None, :], 6, axis=-1)
        csh1, csc1, cg1, csh2, csc2, cg2 = jnp.split(jax.nn.silu(c_ctx) @ w_mod[l] + b_mod[l], 6, axis=-1)
        y, y_ctx = hybrid_mixer(modulate(rms_norm(x), sh1, sc1), modulate(rms_norm(ctx), csh1, csc1),
                                w_in[l], na_q_gain[l], na_k_gain[l], na_rpb[l], m_gate_bias[l], m_norm_gain[l],
                                conv_w[l], conv_b[l], conv_ln_g[l], conv_ln_b[l],
                                w_br_na[l], w_br_m[l], w_br_conv[l], w_out[l], keep_ctx)
        x = x + g1 * y
        x = x + g2 * expert_choice_ffn(modulate(rms_norm(x), sh2, sc2), w_router[l], wg, wu, wd)
        if keep_ctx:
            ctx = ctx + cg1 * y_ctx
            ctx = ctx + cg2 * expert_choice_ffn(modulate(rms_norm(ctx), csh2, csc2), w_router[l], wg, wu, wd)
    return x
```

```python
import functools

import jax
import jax.numpy as jnp
from jax import lax
from jax.experimental import pallas as pl
from jax.experimental.pallas import tpu as pltpu

DEPTH = 4
CTX_LEN = 256
GRID_W = 64
NA_HEADS = 8
NA_HEAD_DIM = 64
NA_KH = 8
NA_KW = 16
M_HEADS = 4
M_HEAD_DIM = 128
CONV_CH = 512
CONV_K = 31
N_EXPERTS = 16
EC_FACTOR = 2
ROPE_BASE = 10000.0
NORM_EPS = 1e-6

NA_W = NA_HEADS * NA_HEAD_DIM
M_W = M_HEADS * M_HEAD_DIM
N_GATE_COLS = 4 * M_HEADS

LANES = 128
TM = 256
M_CH = 128
CONV_HALO = 16
CONV_ROWS = 64
SLOT_WIN = 64
NEG = -1e30
VMEM_LIMIT_BYTES = 56 * 1024 * 1024

F32 = jnp.float32
BF16 = jnp.bfloat16
NT_DIMS = (((1,), (1,)), ((), ()))
TN_DIMS = (((0,), (0,)), ((), ()))


def _cparams(n_axes):
    return pltpu.CompilerParams(dimension_semantics=("arbitrary",) * n_axes,
                                vmem_limit_bytes=VMEM_LIMIT_BYTES)


def _sigmoid(x):
    return 1.0 / (1.0 + jnp.exp(-x))


def _modnorm(x, shift, scale):
    ms = jnp.mean(x * x, axis=-1, keepdims=True)
    return (x * lax.rsqrt(ms + NORM_EPS)) * (1.0 + scale) + shift


def _chunk(width):
    return 512 if width % 512 == 0 else 256


def _mod_kernel(c_ref, w_ref, b_ref, o_ref):
    c = c_ref[...]
    s = (c * _sigmoid(c)).astype(BF16)
    o_ref[0] = jnp.dot(s, w_ref[0].astype(BF16), preferred_element_type=F32) + b_ref[0]


def modulation(cc, w_mod, b_mod):
    depth, D, N = w_mod.shape
    tn = 1024 if N % 1024 == 0 else N
    return pl.pallas_call(
        _mod_kernel,
        grid=(depth, N // tn),
        in_specs=[pl.BlockSpec((16, D), lambda l, n: (0, 0)),
                  pl.BlockSpec((1, D, tn), lambda l, n: (l, 0, n)),
                  pl.BlockSpec((1, 1, tn), lambda l, n: (l, 0, n))],
        out_specs=pl.BlockSpec((1, 16, tn), lambda l, n: (l, 0, n)),
        out_shape=jax.ShapeDtypeStruct((depth, 16, N), F32),
        compiler_params=_cparams(2), name="modulation",
    )(cc, w_mod, b_mod.reshape(depth, 1, N))


def _inproj_kernel(x_ref, mod_ref, w_ref, qg_ref, kg_ref, gm_ref, cos_ref, sa_ref, sb_ref, gb_ref,
                   naq_ref, nak_ref, nav_ref, mq_ref, mk_ref, mv_ref, mo_ref, mg_ref, cu_ref, bg_ref, *, D):
    mod = mod_ref[0, 0]
    h = _modnorm(x_ref[0], mod[:, 0:D], mod[:, D:2 * D]).astype(BF16)

    def proj(off, width):
        return jnp.dot(h, w_ref[:, off:off + width], preferred_element_type=F32)

    gm = gm_ref[...]
    for ref, gain_ref, off in ((naq_ref, qg_ref, 0), (nak_ref, kg_ref, NA_W)):
        y = proj(off, NA_W)
        ysq = y * y
        hi = ysq.astype(BF16)
        lo = (ysq - hi.astype(F32)).astype(BF16)
        ss = jnp.dot(hi, gm, preferred_element_type=F32) + jnp.dot(lo, gm, preferred_element_type=F32)
        ref[0] = (y * lax.rsqrt(ss * (1.0 / NA_HEAD_DIM) + NORM_EPS) * gain_ref[...]).astype(BF16)
    nav_ref[0] = proj(2 * NA_W, NA_W).astype(BF16)

    cos, sa, sb = cos_ref[...], sa_ref[...], sb_ref[...]
    q4 = M_HEAD_DIM // 4
    base = 3 * NA_W
    for ref, off, scale in ((mq_ref, base, 1.0), (mk_ref, base + M_W, M_HEAD_DIM ** -0.5)):
        y = proj(off, M_W)
        for hh in range(M_HEADS):
            yh = y[:, hh * M_HEAD_DIM:(hh + 1) * M_HEAD_DIM]
            r = yh * cos + pltpu.roll(yh, M_HEAD_DIM - q4, 1) * sa + pltpu.roll(yh, q4, 1) * sb
            ref[0, :, hh * M_HEAD_DIM:(hh + 1) * M_HEAD_DIM] = (r * scale).astype(BF16)
    mv_ref[0] = proj(base + 2 * M_W, M_W).astype(BF16)
    mo_ref[0] = proj(base + 3 * M_W, M_W).astype(BF16)
    base += 4 * M_W
    for ref, width in ((cu_ref, 2 * CONV_CH), (bg_ref, 3 * D)):
        cw = _chunk(width)
        for j in range(width // cw):
            ref[0, :, j * cw:(j + 1) * cw] = proj(base + j * cw, cw).astype(BF16)
        base += width
    mg_ref[0] = proj(base, 2 * LANES) + gb_ref[...]


def input_projection(X, modsel, w_packed, q_gain, k_gain, gmat, cos, sina, sinb, gate_bias):
    B, Tt, D = X.shape
    NP = w_packed.shape[1]
    nt = Tt // TM
    tok = lambda w: pl.BlockSpec((1, TM, w), lambda b, t: (b, t, 0))
    const = lambda shape: pl.BlockSpec(shape, lambda b, t: (0,) * len(shape))
    out_widths = (NA_W, NA_W, NA_W, M_W, M_W, M_W, M_W, 2 * LANES, 2 * CONV_CH, 3 * D)
    out_dtypes = (BF16,) * 7 + (F32, BF16, BF16)
    return pl.pallas_call(
        functools.partial(_inproj_kernel, D=D),
        grid=(B, nt),
        in_specs=[tok(D),
                  pl.BlockSpec((1, 1, 1, 6 * D), lambda b, t: (b, jnp.minimum(t, 1), 0, 0)),
                  pl.BlockSpec((D, NP), lambda b, t: (0, 0), pipeline_mode=pl.Buffered(1)),
                  const((1, NA_W)), const((1, NA_W)), const((NA_W, NA_W)),
                  pl.BlockSpec((TM, M_HEAD_DIM), lambda b, t: (t, 0)),
                  pl.BlockSpec((TM, M_HEAD_DIM), lambda b, t: (t, 0)),
                  pl.BlockSpec((TM, M_HEAD_DIM), lambda b, t: (t, 0)),
                  const((1, 2 * LANES))],
        out_specs=[tok(w) for w in out_widths],
        out_shape=[jax.ShapeDtypeStruct((B, Tt, w), dt) for w, dt in zip(out_widths, out_dtypes)],
        compiler_params=_cparams(2), name="input_projection",
    )(X, modsel, w_packed, q_gain, k_gain, gmat, cos, sina, sinb, gate_bias)


def _na_kernel(q_ref, k_ref, v_ref, bias_ref, o_ref, *, L, rows):
    t = pl.program_id(2)
    even = lax.broadcasted_iota(jnp.int32, (1, LANES), 1) < NA_HEAD_DIM
    kc = k_ref[0, 0:L, :]
    vc = v_ref[0, 0:L, :]

    def heads_stack(q):
        z = jnp.zeros_like(q)
        return jnp.concatenate([jnp.where(even, q, z), jnp.where(even, z, q)], axis=0)

    def heads_merge(o):
        n = o.shape[0] // 2
        return jnp.where(even, o[:n], o[n:])

    @pl.when(t == 0)
    def _context_queries():
        q2 = heads_stack(q_ref[0])
        s = lax.dot_general(q2, kc, NT_DIMS, preferred_element_type=F32)
        m = jnp.max(s, axis=-1, keepdims=True)
        p = jnp.exp(s - m)
        l = jnp.sum(p, axis=-1, keepdims=True)
        o = jnp.dot(p.astype(BF16), vc, preferred_element_type=F32) / l
        o_ref[0] = heads_merge(o).astype(BF16)

    @pl.when(t > 0)
    def _latent_queries():
        for rr in range(TM // GRID_W):
            r = (t - 1) * (TM // GRID_W) + rr
            rs = jnp.clip(r - NA_KH // 2, 0, rows - NA_KH)
            start = pl.multiple_of(L + rs * GRID_W, GRID_W)
            kw = k_ref[0, pl.ds(start, NA_KH * GRID_W), :]
            vw = v_ref[0, pl.ds(start, NA_KH * GRID_W), :]
            q2 = heads_stack(q_ref[0, rr * GRID_W:(rr + 1) * GRID_W, :])
            s_lat = lax.dot_general(q2, kw, NT_DIMS, preferred_element_type=F32) + bias_ref[0, r - rs]
            s_ctx = lax.dot_general(q2, kc, NT_DIMS, preferred_element_type=F32)
            m = jnp.maximum(jnp.max(s_lat, axis=-1, keepdims=True), jnp.max(s_ctx, axis=-1, keepdims=True))
            p_lat = jnp.exp(s_lat - m)
            p_ctx = jnp.exp(s_ctx - m)
            l = jnp.sum(p_lat, axis=-1, keepdims=True) + jnp.sum(p_ctx, axis=-1, keepdims=True)
            o = (jnp.dot(p_lat.astype(BF16), vw, preferred_element_type=F32)
                 + jnp.dot(p_ctx.astype(BF16), vc, preferred_element_type=F32)) / l
            o_ref[0, rr * GRID_W:(rr + 1) * GRID_W, :] = heads_merge(o).astype(BF16)


def na_attention(q, k, v, bias):
    B, Tt, _ = q.shape
    L = CTX_LEN
    rows = (Tt - L) // GRID_W
    pairs = NA_W // LANES
    return pl.pallas_call(
        functools.partial(_na_kernel, L=L, rows=rows),
        grid=(B, pairs, Tt // TM),
        in_specs=[pl.BlockSpec((1, TM, LANES), lambda b, j, t: (b, t, j)),
                  pl.BlockSpec((1, Tt, LANES), lambda b, j, t: (b, 0, j)),
                  pl.BlockSpec((1, Tt, LANES), lambda b, j, t: (b, 0, j)),
                  pl.BlockSpec((1,) + bias.shape[1:], lambda b, j, t: (j, 0, 0, 0))],
        out_specs=pl.BlockSpec((1, TM, LANES), lambda b, j, t: (b, t, j)),
        out_shape=jax.ShapeDtypeStruct((B, Tt, NA_W), BF16),
        compiler_params=_cparams(3), name="na_attention",
    )(q, k, v, bias)


def na_bias_table(rpb):
    H = rpb.shape[0]
    cq = jnp.arange(GRID_W)
    col_start = jnp.clip(cq - NA_KW // 2, 0, GRID_W - NA_KW)
    col_mask = (cq[None, :] >= col_start[:, None]) & (cq[None, :] < col_start[:, None] + NA_KW)
    col_idx = jnp.clip(cq[None, :] - cq[:, None] + NA_KW - 1, 0, 2 * NA_KW - 2)
    row_idx = jnp.arange(NA_KH)[None, :] - jnp.arange(NA_KH)[:, None] + NA_KH - 1
    tbl = rpb[:, row_idx[:, :, None, None], col_idx[None, None, :, :]]
    tbl = jnp.where(col_mask[None, None, None], tbl, NEG)
    tbl = jnp.transpose(tbl, (0, 1, 3, 2, 4)).reshape(H // 2, 2, NA_KH, GRID_W, NA_KH * GRID_W)
    return jnp.transpose(tbl, (0, 2, 1, 3, 4)).reshape(H // 2, NA_KH, 2 * GRID_W, NA_KH * GRID_W)


def _mlstm_kernel(q_ref, k_ref, v_ref, g_ref, o_ref, ct_ref, m_ref):
    d = pl.program_id(0)

    @pl.when(pl.program_id(2) == 0)
    def _():
        ct_ref[...] = jnp.zeros_like(ct_ref)
        m_ref[...] = jnp.zeros_like(m_ref)

    n = M_CH
    G = g_ref[0]
    GT = G.T
    LFT = jnp.minimum(GT, 0.0) - jnp.log(1.0 + jnp.exp(-jnp.abs(GT)))
    lane = lax.broadcasted_iota(jnp.int32, (n, n), 1)
    sub = lax.broadcasted_iota(jnp.int32, (n, n), 0)
    pre = LFT
    sh = 1
    while sh < n:
        pre = pre + jnp.where(lane >= sh, pltpu.roll(pre, sh, 1), 0.0)
        sh *= 2
    tot_all = pre[:, n - 1:n]
    BT = jnp.where(d == 0, pre, tot_all - pre + LFT)
    Bc = BT.T
    causal = ((lane - sub) * (1 - 2 * d)) <= 0
    one0 = jnp.where(lane == 0, 1.0, 0.0).astype(BF16)

    for hh in range(M_HEADS):
        fc = M_HEADS + hh
        b_col, i_col = Bc[:, fc:fc + 1], G[:, hh:hh + 1]
        b_row, i_row = BT[fc:fc + 1, :], GT[hh:hh + 1, :]
        tot = tot_all[fc:fc + 1, :]
        m0 = m_ref[hh][:, 0:1]
        a_row = tot - b_row + i_row
        m_new = jnp.maximum(tot + m0, jnp.max(a_row, axis=1, keepdims=True))
        decay = jnp.exp(tot + m0 - m_new)
        w_col = jnp.exp(tot - b_col + i_col - m_new)
        dm = jnp.where(causal, b_col - b_row + i_row, NEG)
        m_col = jnp.maximum(b_col + m0, jnp.max(dm, axis=1, keepdims=True))
        e_col = jnp.exp(b_col + m0 - m_col)
        sl = slice(hh * M_HEAD_DIM, (hh + 1) * M_HEAD_DIM)
        qh, kh, vh = q_ref[0, :, sl], k_ref[0, :, sl], v_ref[0, :, sl]
        s = lax.dot_general(qh, kh, NT_DIMS, preferred_element_type=F32) * jnp.exp(dm - m_col)
        vaug = jnp.concatenate([vh, one0], axis=1)
        ct = ct_ref[hh]
        num = (e_col * jnp.dot(qh, ct.astype(BF16), preferred_element_type=F32)
               + jnp.dot(s.astype(BF16), vaug, preferred_element_type=F32))
        den = num[:, M_HEAD_DIM:M_HEAD_DIM + 1]
        o_ref[0, 0, :, sl] = num[:, :M_HEAD_DIM] / jnp.maximum(jnp.abs(den), jnp.exp(-m_col))
        wv = (w_col * vaug.astype(F32)).astype(BF16)
        ct_ref[hh] = decay * ct + lax.dot_general(kh, wv, TN_DIMS, preferred_element_type=F32)
        m_ref[hh] = jnp.broadcast_to(m_new, (1, LANES))


def mlstm(q, k, v, g):
    B, Tt, _ = q.shape
    nc = Tt // M_CH
    lc = CTX_LEN // M_CH

    def blk(d, s):
        return jnp.where(d == 0, s, jnp.where(s < lc, lc - 1 - s, nc - 1 + lc - s))

    tok = pl.BlockSpec((1, M_CH, M_W), lambda d, b, s: (b, blk(d, s), 0))
    return pl.pallas_call(
        _mlstm_kernel,
        grid=(2, B, nc),
        in_specs=[tok, tok, tok, pl.BlockSpec((1, M_CH, LANES), lambda d, b, s: (b, blk(d, s), d))],
        out_specs=pl.BlockSpec((1, 1, M_CH, M_W), lambda d, b, s: (d, b, blk(d, s), 0)),
        out_shape=jax.ShapeDtypeStruct((2, B, Tt, M_W), F32),
        scratch_shapes=[pltpu.VMEM((M_HEADS, M_HEAD_DIM, M_HEAD_DIM + LANES), F32),
                        pltpu.VMEM((M_HEADS, 1, LANES), F32)],
        compiler_params=_cparams(3), name="mlstm",
    )(q, k, v, g)


def _conv_kernel(cur_ref, prev_ref, next_ref, w_ref, b_ref, g_ref, be_ref, o_ref, hbuf, *, nt):
    t = pl.program_id(1)
    C = CONV_CH

    def glu(u):
        u = u.astype(F32)
        return u[:, :C] * _sigmoid(u[:, C:])

    prev_ok = t >= 2
    next_ok = jnp.logical_and(t >= 1, t < nt - 1)
    hbuf[0:CONV_HALO] = jnp.where(prev_ok, glu(prev_ref[0]), 0.0)
    hbuf[CONV_HALO:CONV_HALO + TM] = glu(cur_ref[0])
    hbuf[CONV_HALO + TM:2 * CONV_HALO + TM] = jnp.where(next_ok, glu(next_ref[0]), 0.0)
    off = CONV_HALO - CONV_K // 2
    for rc in range(TM // CONV_ROWS):
        acc = jnp.zeros((CONV_ROWS, C), F32) + b_ref[...]
        for j in range(CONV_K):
            acc = acc + hbuf[pl.ds(rc * CONV_ROWS + off + j, CONV_ROWS), :] * w_ref[j:j + 1, :]
        mu = jnp.mean(acc, axis=-1, keepdims=True)
        xc = acc - mu
        var = jnp.mean(xc * xc, axis=-1, keepdims=True)
        y = xc * lax.rsqrt(var + NORM_EPS) * g_ref[...] + be_ref[...]
        o_ref[0, rc * CONV_ROWS:(rc + 1) * CONV_ROWS, :] = (y * _sigmoid(y)).astype(BF16)


def conv_module(cu, conv_w, conv_b, ln_g, ln_b):
    B, Tt, W = cu.shape
    nt = Tt // TM
    hb = TM // CONV_HALO
    nhb = Tt // CONV_HALO
    C = CONV_CH
    wpad = jnp.zeros((32, C), F32).at[:CONV_K].set(conv_w)
    vec = lambda a: a.reshape(1, C)
    const = lambda shape: pl.BlockSpec(shape, lambda b, t: (0,) * len(shape))
    return pl.pallas_call(
        functools.partial(_conv_kernel, nt=nt),
        grid=(B, nt),
        in_specs=[pl.BlockSpec((1, TM, W), lambda b, t: (b, t, 0)),
                  pl.BlockSpec((1, CONV_HALO, W), lambda b, t: (b, jnp.maximum(t * hb - 1, 0), 0)),
                  pl.BlockSpec((1, CONV_HALO, W), lambda b, t: (b, jnp.minimum((t + 1) * hb, nhb - 1), 0)),
                  const((32, C)), const((1, C)), const((1, C)), const((1, C))],
        out_specs=pl.BlockSpec((1, TM, C), lambda b, t: (b, t, 0)),
        out_shape=jax.ShapeDtypeStruct((B, Tt, C), BF16),
        scratch_shapes=[pltpu.VMEM((TM + 2 * CONV_HALO, C), F32)],
        compiler_params=_cparams(2), name="conv_module",
    )(cu, cu, cu, wpad, vec(conv_b), vec(ln_g), vec(ln_b))


def _merge_kernel(na_ref, hfw_ref, hbw_ref, mo_ref, cv_ref, bg_ref, x_ref, mod_ref, ng_ref,
                  wna_ref, wm_ref, wc_ref, wo_ref, wr_ref, x1_ref, h2_ref, aff_ref, *, D):
    mod = mod_ref[0, 0]
    mh = hfw_ref[0, 0] + hbw_ref[0, 0]
    parts = []
    for hh in range(M_HEADS):
        xh = mh[:, hh * M_HEAD_DIM:(hh + 1) * M_HEAD_DIM]
        parts.append(xh * lax.rsqrt(jnp.mean(xh * xh, axis=-1, keepdims=True) + NORM_EPS))
    mn = (jnp.concatenate(parts, axis=1) * ng_ref[...] * _sigmoid(mo_ref[0].astype(F32))).astype(BF16)
    z = (_sigmoid(bg_ref[0, :, 0:D].astype(F32)) * jnp.dot(na_ref[0], wna_ref[...], preferred_element_type=F32)
         + _sigmoid(bg_ref[0, :, D:2 * D].astype(F32)) * jnp.dot(mn, wm_ref[...], preferred_element_type=F32)
         + _sigmoid(bg_ref[0, :, 2 * D:3 * D].astype(F32)) * jnp.dot(cv_ref[0], wc_ref[...], preferred_element_type=F32))
    y = jnp.dot(z.astype(BF16), wo_ref[...], preferred_element_type=F32)
    x1 = x_ref[0] + mod[:, 2 * D:3 * D] * y
    x1_ref[0] = x1
    h2 = _modnorm(x1, mod[:, 3 * D:4 * D], mod[:, 4 * D:5 * D]).astype(BF16)
    h2_ref[0] = h2
    logits = lax.dot_general(wr_ref[...], h2, NT_DIMS, preferred_element_type=F32)
    e = jnp.exp(logits - jnp.max(logits, axis=0, keepdims=True))
    aff_ref[0] = e / jnp.sum(e, axis=0, keepdims=True)


def merge_and_route(na, hm, mo, cv, bg, X, modsel, norm_gain, w_na, w_m, w_c, w_o, w_rt):
    B, Tt, D = X.shape
    E = w_rt.shape[0]
    nt = Tt // TM
    tok = lambda w: pl.BlockSpec((1, TM, w), lambda b, t: (b, t, 0))
    const = lambda shape: pl.BlockSpec(shape, lambda b, t: (0,) * len(shape))
    return pl.pallas_call(
        functools.partial(_merge_kernel, D=D),
        grid=(B, nt),
        in_specs=[tok(NA_W),
                  pl.BlockSpec((1, 1, TM, M_W), lambda b, t: (0, b, t, 0)),
                  pl.BlockSpec((1, 1, TM, M_W), lambda b, t: (1, b, t, 0)),
                  tok(M_W), tok(CONV_CH), tok(3 * D), tok(D),
                  pl.BlockSpec((1, 1, 1, 6 * D), lambda b, t: (b, jnp.minimum(t, 1), 0, 0)),
                  const((1, M_W)), const((NA_W, D)), const((M_W, D)), const((CONV_CH, D)), const((D, D)),
                  const((E, D))],
        out_specs=[tok(D), tok(D), pl.BlockSpec((1, E, TM), lambda b, t: (b, 0, t))],
        out_shape=[jax.ShapeDtypeStruct((B, Tt, D), F32), jax.ShapeDtypeStruct((B, Tt, D), BF16),
                   jax.ShapeDtypeStruct((B, E, Tt), F32)],
        compiler_params=_cparams(2), name="merge_and_route",
    )(na, hm, hm, mo, cv, bg, X, modsel, norm_gain, w_na, w_m, w_c, w_o, w_rt)


def _topk_kernel(aff_ref, pos_ref, cnt_ref, *, L, k_ctx, k_lat):
    E, Tt = aff_ref.shape[1:]
    u = pltpu.bitcast(aff_ref[0], jnp.int32)
    is_ctx = lax.broadcasted_iota(jnp.int32, (E, Tt), 1) < L

    def count(mask):
        f = jnp.where(mask, 1.0, 0.0)
        c_ctx = jnp.sum(jnp.where(is_ctx, f, 0.0), axis=1, keepdims=True)
        return c_ctx, jnp.sum(f, axis=1, keepdims=True) - c_ctx

    def bit_step(i, carry):
        v_ctx, v_lat = carry
        bit = jnp.left_shift(jnp.int32(1), 30 - i)
        c_ctx, c_lat = count(u >= jnp.where(is_ctx, v_ctx | bit, v_lat | bit))
        return (jnp.where(c_ctx >= k_ctx, v_ctx | bit, v_ctx), jnp.where(c_lat >= k_lat, v_lat | bit, v_lat))

    zero = jnp.zeros((E, 1), jnp.int32)
    v_ctx, v_lat = lax.fori_loop(0, 31, bit_step, (zero, zero))
    thr = jnp.where(is_ctx, v_ctx, v_lat)
    g_ctx, g_lat = count(u > thr)
    need_ctx, need_lat = k_ctx - g_ctx, k_lat - g_lat

    r = lax.broadcasted_iota(jnp.int32, (TM, TM), 0)
    c = lax.broadcasted_iota(jnp.int32, (TM, TM), 1)
    tri = jnp.where(r < c, 1.0, 0.0).astype(BF16)
    lane_t = lax.broadcasted_iota(jnp.int32, (E, LANES), 1)
    cnt = jnp.zeros((E, LANES), F32)
    ties = jnp.zeros((E, 1), F32)
    slots = jnp.zeros((E, 1), F32)
    nt = Tt // TM
    for t in range(nt):
        if t == 1:
            ties = jnp.zeros((E, 1), F32)
        need = need_ctx if t == 0 else need_lat
        ut = pltpu.bitcast(aff_ref[0, :, t * TM:(t + 1) * TM], jnp.int32)
        tt = v_ctx if t == 0 else v_lat
        eq = jnp.where(ut == tt, 1.0, 0.0)
        rank = ties + jnp.dot(eq.astype(BF16), tri, preferred_element_type=F32)
        sel = jnp.where(ut > tt, 1.0, jnp.where(rank < need, eq, 0.0))
        pos = slots + jnp.dot(sel.astype(BF16), tri, preferred_element_type=F32)
        pos_ref[0, :, t * TM:(t + 1) * TM] = jnp.where(sel > 0.0, pos, -1.0).astype(jnp.int32)
        cnt = jnp.where(lane_t == t, slots, cnt)
        ties = ties + jnp.sum(eq, axis=1, keepdims=True)
        slots = slots + jnp.sum(sel, axis=1, keepdims=True)
    cnt = jnp.where(lane_t == nt, slots, cnt)
    cnt_ref[0] = cnt.astype(jnp.int32)


def expert_choice_slots(aff):
    B, E, Tt = aff.shape
    L = CTX_LEN
    k_ctx = EC_FACTOR * L // N_EXPERTS
    k_lat = EC_FACTOR * (Tt - L) // N_EXPERTS
    assert Tt // TM + 1 <= LANES
    return pl.pallas_call(
        functools.partial(_topk_kernel, L=L, k_ctx=k_ctx, k_lat=k_lat),
        grid=(B,),
        in_specs=[pl.BlockSpec((1, E, Tt), lambda b: (b, 0, 0))],
        out_specs=[pl.BlockSpec((1, E, Tt), lambda b: (b, 0, 0)), pl.BlockSpec((1, E, LANES), lambda b: (b, 0, 0))],
        out_shape=[jax.ShapeDtypeStruct((B, E, Tt), jnp.int32), jax.ShapeDtypeStruct((B, E, LANES), jnp.int32)],
        compiler_params=_cparams(1), name="expert_choice_slots",
    )(aff)


def _slot_tile(ns):
    for n in (6, 5, 4, 3, 2, 1):
        if ns % (16 * n) == 0:
            return ns // n
    raise ValueError(ns)


def _ffn_kernel(cnt_ref, h_ref, pos_ref, wg_ref, wu_ref, wd_ref, y_ref, xin_ref, acc_ref, *, ns, st, nt):
    b, e, f = pl.program_id(0), pl.program_id(1), pl.program_id(2)

    @pl.when(f == 0)
    def _gather():
        xin_ref[...] = jnp.zeros_like(xin_ref)
        acc_ref[...] = jnp.zeros_like(acc_ref)

        def tile_body(t, carry):
            lo, hi = cnt_ref[b, e, t], cnt_ref[b, e, t + 1]
            tok0 = pl.multiple_of(t * TM, TM)
            for j in range(ns // st):
                @pl.when(jnp.logical_and(lo < (j + 1) * st, hi > j * st))
                def _():
                    srow = j * st + lax.broadcasted_iota(jnp.int32, (st, TM), 0)
                    onehot = jnp.where(pos_ref[0, 0, :, pl.ds(tok0, TM)] == srow, 1.0, 0.0).astype(BF16)
                    rows = jnp.dot(onehot, h_ref[0, pl.ds(tok0, TM), :], preferred_element_type=F32)
                    xin_ref[j * st:(j + 1) * st, :] = (xin_ref[j * st:(j + 1) * st, :].astype(F32) + rows).astype(BF16)
            return carry

        lax.fori_loop(0, nt, tile_body, 0)

    x = xin_ref[...]
    a = jnp.dot(x, wg_ref[0], preferred_element_type=F32)
    u = jnp.dot(x, wu_ref[0], preferred_element_type=F32)
    hmid = (a * _sigmoid(a) * u).astype(BF16)
    acc_ref[...] += jnp.dot(hmid, wd_ref[0], preferred_element_type=F32)

    @pl.when(f == pl.num_programs(2) - 1)
    def _():
        y_ref[0, 0, 0:ns, :] = acc_ref[...].astype(BF16)
        y_ref[0, 0, ns:ns + SLOT_WIN, :] = jnp.zeros((SLOT_WIN, y_ref.shape[-1]), BF16)


def expert_ffn(cnt, h2, pos, w_gate, w_up, w_down):
    B, Tt, D = h2.shape
    E, _, F = w_gate.shape
    L = CTX_LEN
    ns = EC_FACTOR * L // N_EXPERTS + EC_FACTOR * (Tt - L) // N_EXPERTS
    st = _slot_tile(ns)
    tf = min(512, F)
    grid_spec = pltpu.PrefetchScalarGridSpec(
        num_scalar_prefetch=1,
        grid=(B, E, F // tf),
        in_specs=[pl.BlockSpec((1, Tt, D), lambda b, e, f, c: (b, 0, 0), pipeline_mode=pl.Buffered(1)),
                  pl.BlockSpec((1, 1, 1, Tt), lambda b, e, f, c: (b, e, 0, 0)),
                  pl.BlockSpec((1, D, tf), lambda b, e, f, c: (e, 0, f)),
                  pl.BlockSpec((1, D, tf), lambda b, e, f, c: (e, 0, f)),
                  pl.BlockSpec((1, tf, D), lambda b, e, f, c: (e, f, 0))],
        out_specs=pl.BlockSpec((1, 1, ns + SLOT_WIN, D), lambda b, e, f, c: (b, e, 0, 0)),
        scratch_shapes=[pltpu.VMEM((ns, D), BF16), pltpu.VMEM((ns, D), F32)])
    return pl.pallas_call(
        functools.partial(_ffn_kernel, ns=ns, st=st, nt=Tt // TM),
        grid_spec=grid_spec,
        out_shape=jax.ShapeDtypeStruct((B, E, ns + SLOT_WIN, D), BF16),
        compiler_params=_cparams(3), name="expert_ffn",
    )(cnt, h2, pos.reshape(B, E, 1, Tt), w_gate, w_up, w_down)


def _combine_kernel(cnt_ref, pos_ref, aff_ref, y_ref, x_ref, mod_ref, o_ref, yw_ref, p_ref, acc_ref, *, D, ns):
    b, t = pl.program_id(0), pl.program_id(1)
    E = pos_ref.shape[1]
    W = SLOT_WIN
    starts, rounds = [], jnp.int32(0)
    for e in range(E):
        lo, hi = cnt_ref[b, e, t], cnt_ref[b, e, t + 1]
        s0 = (lo // 16) * 16
        starts.append(s0)
        rounds = jnp.maximum(rounds, jnp.where(hi > lo, (hi - s0 + W - 1) // W, 0))
    acc_ref[...] = jnp.zeros_like(acc_ref)

    def round_body(r, carry):
        for e in range(E):
            base = pl.multiple_of(jnp.minimum(starts[e] + r * W, ns), 16)
            yw_ref[e * W:(e + 1) * W, :] = y_ref[0, e, pl.ds(base, W), :]
            srow = base + lax.broadcasted_iota(jnp.int32, (W, TM), 0)
            p_ref[e * W:(e + 1) * W, :] = jnp.where(pos_ref[0, e:e + 1, :] == srow, aff_ref[0, e:e + 1, :], 0.0).astype(BF16)
        acc_ref[...] += lax.dot_general(p_ref[...], yw_ref[...], TN_DIMS, preferred_element_type=F32)
        return carry

    lax.fori_loop(0, rounds, round_body, 0)
    o_ref[0] = x_ref[0] + mod_ref[0, 0][:, 5 * D:6 * D] * acc_ref[...]


def combine(cnt, pos, aff, Y, X1, modsel):
    B, Tt, D = X1.shape
    E = pos.shape[1]
    nsp = Y.shape[2]
    grid_spec = pltpu.PrefetchScalarGridSpec(
        num_scalar_prefetch=1,
        grid=(B, Tt // TM),
        in_specs=[pl.BlockSpec((1, E, TM), lambda b, t, c: (b, 0, t)),
                  pl.BlockSpec((1, E, TM), lambda b, t, c: (b, 0, t)),
                  pl.BlockSpec((1, E, nsp, D), lambda b, t, c: (b, 0, 0, 0), pipeline_mode=pl.Buffered(1)),
                  pl.BlockSpec((1, TM, D), lambda b, t, c: (b, t, 0)),
                  pl.BlockSpec((1, 1, 1, 6 * D), lambda b, t, c: (b, jnp.minimum(t, 1), 0, 0))],
        out_specs=pl.BlockSpec((1, TM, D), lambda b, t, c: (b, t, 0)),
        scratch_shapes=[pltpu.VMEM((E * SLOT_WIN, D), BF16), pltpu.VMEM((E * SLOT_WIN, TM), BF16),
                        pltpu.VMEM((TM, D), F32)])
    return pl.pallas_call(
        functools.partial(_combine_kernel, D=D, ns=nsp - SLOT_WIN),
        grid_spec=grid_spec,
        out_shape=jax.ShapeDtypeStruct((B, Tt, D), F32),
        compiler_params=_cparams(2), name="combine",
    )(cnt, pos, aff, Y, X1, modsel)


def _pack_w_in(w, D):
    widths = (NA_W, NA_W, NA_W, M_W, M_W, M_W, M_W, N_GATE_COLS, 2 * CONV_CH, 3 * D)
    offs = [0]
    for wd in widths:
        offs.append(offs[-1] + wd)
    seg = [w[:, offs[i]:offs[i + 1]] for i in range(len(widths))]
    half = N_GATE_COLS // 2
    pad = jnp.zeros((w.shape[0], LANES - half), w.dtype)
    gates = [seg[7][:, :half], pad, seg[7][:, half:], pad]
    return jnp.concatenate(seg[:7] + seg[8:] + gates, axis=1).astype(BF16)


def _pack_gate_bias(gb):
    half = N_GATE_COLS // 2
    pad = jnp.zeros((LANES - half,), F32)
    return jnp.concatenate([gb[:half], pad, gb[half:], pad]).reshape(1, 2 * LANES)


def _rope_tables(T, L):
    nf = M_HEAD_DIM // 4
    t = jnp.arange(T)
    row = (t // GRID_W).astype(F32)
    col = (t % GRID_W).astype(F32)
    inv = ROPE_BASE ** (-jnp.arange(nf, dtype=F32) / nf)
    ar, ac = row[:, None] * inv[None, :], col[:, None] * inv[None, :]
    z = jnp.zeros((T, nf), F32)
    cos = jnp.concatenate([jnp.cos(ar), jnp.cos(ar), jnp.cos(ac), jnp.cos(ac)], axis=1)
    sina = jnp.concatenate([-jnp.sin(ar), z, -jnp.sin(ac), z], axis=1)
    sinb = jnp.concatenate([z, jnp.sin(ar), z, jnp.sin(ac)], axis=1)
    ident = lambda v: jnp.full((L, M_HEAD_DIM), v, F32)
    return (jnp.concatenate([ident(1.0), cos]), jnp.concatenate([ident(0.0), sina]),
            jnp.concatenate([ident(0.0), sinb]))


def kernel(x, c, ctx, c_ctx, w_mod, b_mod, w_in, na_q_gain, na_k_gain, na_rpb, m_gate_bias,
           m_norm_gain, conv_w, conv_b, conv_ln_g, conv_ln_b, w_br_na, w_br_m, w_br_conv, w_out,
           w_router, w_gate, w_up, w_down):
    B, T, D = x.shape
    L = ctx.shape[1]
    assert L == CTX_LEN == TM and T % TM == 0 and B + 1 <= 16
    depth = w_in.shape[0]
    X = jnp.concatenate([ctx, x], axis=1)
    cc = jnp.zeros((16, D), F32).at[:B].set(c).at[B].set(c_ctx)
    mods = modulation(cc, w_mod, b_mod)
    cos, sina, sinb = _rope_tables(T, L)
    hd = jnp.arange(NA_W) // NA_HEAD_DIM
    gmat = (hd[:, None] == hd[None, :]).astype(BF16)
    for l in range(depth):
        modsel = jnp.stack([jnp.broadcast_to(mods[l, B], (B, 6 * D)), mods[l, :B]], axis=1).reshape(B, 2, 1, 6 * D)
        q_gain = (jnp.tile(na_q_gain[l], NA_HEADS) * NA_HEAD_DIM ** -0.5).reshape(1, NA_W)
        k_gain = jnp.tile(na_k_gain[l], NA_HEADS).reshape(1, NA_W)
        naq, nak, nav, mq, mk, mv, mo, mg, cu, bg = input_projection(
            X, modsel, _pack_w_in(w_in[l], D), q_gain, k_gain, gmat, cos, sina, sinb, _pack_gate_bias(m_gate_bias[l]))
        na = na_attention(naq, nak, nav, na_bias_table(na_rpb[l]))
        hm = mlstm(mq, mk, mv, mg)
        cv = conv_module(cu, conv_w[l], conv_b[l], conv_ln_g[l], conv_ln_b[l])
        X1, h2, aff = merge_and_route(
            na, hm, mo, cv, bg, X, modsel, m_norm_gain[l].reshape(1, M_W),
            w_br_na[l].astype(BF16), w_br_m[l].astype(BF16), w_br_conv[l].astype(BF16), w_out[l].astype(BF16),
            w_router[l].T.astype(BF16))
        pos, cnt = expert_choice_slots(aff)
        cnt = cnt[:, :, :T // TM + 2]
        Y = expert_ffn(cnt, h2, pos, w_gate[l].astype(BF16), w_up[l].astype(BF16), w_down[l].astype(BF16))
        X = combine(cnt, pos, aff, Y, X1, modsel)
    return X[:, L:, :]
```

```python
import functools

import jax
import jax.numpy as jnp
from jax import lax
from jax.experimental import pallas as pl
from jax.experimental.pallas import tpu as pltpu

DEPTH = 4
CTX_LEN = 256
GRID_W = 64
NA_HEADS = 8
NA_HEAD_DIM = 64
NA_KH = 8
NA_KW = 16
M_HEADS = 4
M_HEAD_DIM = 128
CONV_CH = 512
CONV_K = 31
N_EXPERTS = 16
EC_FACTOR = 2
ROPE_BASE = 10000.0
NORM_EPS = 1e-6

NA_W = NA_HEADS * NA_HEAD_DIM
M_W = M_HEADS * M_HEAD_DIM
N_GATE_COLS = 4 * M_HEADS

LANES = 128
TM = 256
M_CH = 128
CONV_HALO = 16
CONV_ROWS = 64
SLOT_WIN = 64
NEG = -1e30
VMEM_LIMIT_BYTES = 56 * 1024 * 1024

F32 = jnp.float32
BF16 = jnp.bfloat16
NT_DIMS = (((1,), (1,)), ((), ()))
TN_DIMS = (((0,), (0,)), ((), ()))


def _cparams(n_axes):
    return pltpu.CompilerParams(dimension_semantics=("arbitrary",) * n_axes,
                                vmem_limit_bytes=VMEM_LIMIT_BYTES)


def _sigmoid(x):
    return 1.0 / (1.0 + jnp.exp(-x))


def _modnorm(x, shift, scale):
    ms = jnp.mean(x * x, axis=-1, keepdims=True)
    return (x * lax.rsqrt(ms + NORM_EPS)) * (1.0 + scale) + shift


def _chunk(width):
    return 512 if width % 512 == 0 else 256


def _mod_kernel(c_ref, w_ref, b_ref, o_ref):
    c = c_ref[...]
    s = (c * _sigmoid(c)).astype(BF16)
    o_ref[0] = jnp.dot(s, w_ref[0].astype(BF16), preferred_element_type=F32) + b_ref[0]


def modulation(cc, w_mod, b_mod):
    depth, D, N = w_mod.shape
    tn = 1024 if N % 1024 == 0 else N
    return pl.pallas_call(
        _mod_kernel,
        grid=(depth, N // tn),
        in_specs=[pl.BlockSpec((16, D), lambda l, n: (0, 0)),
                  pl.BlockSpec((1, D, tn), lambda l, n: (l, 0, n)),
                  pl.BlockSpec((1, 1, tn), lambda l, n: (l, 0, n))],
        out_specs=pl.BlockSpec((1, 16, tn), lambda l, n: (l, 0, n)),
        out_shape=jax.ShapeDtypeStruct((depth, 16, N), F32),
        compiler_params=_cparams(2), name="modulation",
    )(cc, w_mod, b_mod.reshape(depth, 1, N))


def _inproj_kernel(x_ref, mod_ref, w_ref, qg_ref, kg_ref, gm_ref, cos_ref, sa_ref, sb_ref, wgt_ref, gb_ref,
                   naq_ref, nak_ref, nav_ref, mq_ref, mk_ref, mv_ref, mo_ref, mg_ref, cu_ref, bg_ref, *, D):
    mod = mod_ref[0, 0]
    h = _modnorm(x_ref[0], mod[:, 0:D], mod[:, D:2 * D]).astype(BF16)

    def proj(off, width):
        return jnp.dot(h, w_ref[:, off:off + width], preferred_element_type=F32)

    gm = gm_ref[...]
    for ref, gain_ref, off in ((naq_ref, qg_ref, 0), (nak_ref, kg_ref, NA_W)):
        y = proj(off, NA_W)
        ysq = y * y
        hi = ysq.astype(BF16)
        lo = (ysq - hi.astype(F32)).astype(BF16)
        ss = jnp.dot(hi, gm, preferred_element_type=F32) + jnp.dot(lo, gm, preferred_element_type=F32)
        ref[0] = (y * lax.rsqrt(ss * (1.0 / NA_HEAD_DIM) + NORM_EPS) * gain_ref[...]).astype(BF16)
    nav_ref[0] = proj(2 * NA_W, NA_W).astype(BF16)

    cos, sa, sb = cos_ref[...], sa_ref[...], sb_ref[...]
    q4 = M_HEAD_DIM // 4
    base = 3 * NA_W
    for ref, off, scale in ((mq_ref, base, 1.0), (mk_ref, base + M_W, M_HEAD_DIM ** -0.5)):
        y = proj(off, M_W)
        for hh in range(M_HEADS):
            yh = y[:, hh * M_HEAD_DIM:(hh + 1) * M_HEAD_DIM]
            r = yh * cos + pltpu.roll(yh, M_HEAD_DIM - q4, 1) * sa + pltpu.roll(yh, q4, 1) * sb
            ref[0, :, hh * M_HEAD_DIM:(hh + 1) * M_HEAD_DIM] = (r * scale).astype(BF16)
    mv_ref[0] = proj(base + 2 * M_W, M_W).astype(BF16)
    mo_ref[0] = proj(base + 3 * M_W, M_W).astype(BF16)
    base += 4 * M_W
    for ref, width in ((cu_ref, 2 * CONV_CH), (bg_ref, 3 * D)):
        cw = _chunk(width)
        for j in range(width // cw):
            ref[0, :, j * cw:(j + 1) * cw] = proj(base + j * cw, cw).astype(BF16)
        base += width
    mg_ref[0] = lax.dot_general(wgt_ref[...], h, NT_DIMS, preferred_element_type=F32) + gb_ref[...]


def input_projection(X, modsel, w_packed, q_gain, k_gain, gmat, cos, sina, sinb, w_gates_t, gate_bias):
    B, Tt, D = X.shape
    NP = w_packed.shape[1]
    nt = Tt // TM
    tok = lambda w: pl.BlockSpec((1, TM, w), lambda b, t: (b, t, 0))
    const = lambda shape: pl.BlockSpec(shape, lambda b, t: (0,) * len(shape))
    out_widths = (NA_W, NA_W, NA_W, M_W, M_W, M_W, M_W, None, 2 * CONV_CH, 3 * D)
    gate_spec = pl.BlockSpec((1, N_GATE_COLS, TM), lambda b, t: (b, 0, t))
    return pl.pallas_call(
        functools.partial(_inproj_kernel, D=D),
        grid=(B, nt),
        in_specs=[tok(D),
                  pl.BlockSpec((1, 1, 1, 6 * D), lambda b, t: (b, jnp.minimum(t, 1), 0, 0)),
                  pl.BlockSpec((D, NP), lambda b, t: (0, 0), pipeline_mode=pl.Buffered(1)),
                  const((1, NA_W)), const((1, NA_W)), const((NA_W, NA_W)),
                  pl.BlockSpec((TM, M_HEAD_DIM), lambda b, t: (t, 0)),
                  pl.BlockSpec((TM, M_HEAD_DIM), lambda b, t: (t, 0)),
                  pl.BlockSpec((TM, M_HEAD_DIM), lambda b, t: (t, 0)),
                  const((N_GATE_COLS, D)), const((N_GATE_COLS, 1))],
        out_specs=[gate_spec if w is None else tok(w) for w in out_widths],
        out_shape=[jax.ShapeDtypeStruct((B, N_GATE_COLS, Tt), F32) if w is None
                   else jax.ShapeDtypeStruct((B, Tt, w), BF16) for w in out_widths],
        compiler_params=_cparams(2), name="input_projection",
    )(X, modsel, w_packed, q_gain, k_gain, gmat, cos, sina, sinb, w_gates_t, gate_bias)


def _na_kernel(q_ref, k_ref, v_ref, bias_ref, o_ref, *, L, rows):
    t = pl.program_id(2)
    even = lax.broadcasted_iota(jnp.int32, (1, LANES), 1) < NA_HEAD_DIM
    pairs = [slice(jj * LANES, (jj + 1) * LANES) for jj in range(NA_PAIRS)]

    def heads_stack(q):
        z = jnp.zeros_like(q)
        return jnp.concatenate([jnp.where(even, q, z), jnp.where(even, z, q)], axis=0)

    def heads_merge(o):
        n = o.shape[0] // 2
        return jnp.where(even, o[:n], o[n:])

    @pl.when(t == 0)
    def _context_queries():
        for sl in pairs:
            kc, vc = k_ref[0, 0:L, sl], v_ref[0, 0:L, sl]
            q2 = heads_stack(q_ref[0, :, sl])
            s = lax.dot_general(q2, kc, NT_DIMS, preferred_element_type=F32)
            m = jnp.max(s, axis=-1, keepdims=True)
            p = jnp.exp(s - m)
            l = jnp.sum(p, axis=-1, keepdims=True)
            o = jnp.dot(p.astype(BF16), vc, preferred_element_type=F32) / l
            o_ref[0, :, sl] = heads_merge(o).astype(BF16)

    @pl.when(t > 0)
    def _latent_queries():
        for rr in range(TM // GRID_W):
            r = (t - 1) * (TM // GRID_W) + rr
            rs = jnp.clip(r - NA_KH // 2, 0, rows - NA_KH)
            start = pl.multiple_of(L + rs * GRID_W, GRID_W)
            qrows = slice(rr * GRID_W, (rr + 1) * GRID_W)
            for jj, sl in enumerate(pairs):
                kc, vc = k_ref[0, 0:L, sl], v_ref[0, 0:L, sl]
                kw = k_ref[0, pl.ds(start, NA_KH * GRID_W), sl]
                vw = v_ref[0, pl.ds(start, NA_KH * GRID_W), sl]
                q2 = heads_stack(q_ref[0, qrows, sl])
                s_lat = lax.dot_general(q2, kw, NT_DIMS, preferred_element_type=F32) + bias_ref[jj, r - rs]
                s_ctx = lax.dot_general(q2, kc, NT_DIMS, preferred_element_type=F32)
                m = jnp.maximum(jnp.max(s_lat, axis=-1, keepdims=True), jnp.max(s_ctx, axis=-1, keepdims=True))
                p_lat = jnp.exp(s_lat - m)
                p_ctx = jnp.exp(s_ctx - m)
                l = jnp.sum(p_lat, axis=-1, keepdims=True) + jnp.sum(p_ctx, axis=-1, keepdims=True)
                o = (jnp.dot(p_lat.astype(BF16), vw, preferred_element_type=F32)
                     + jnp.dot(p_ctx.astype(BF16), vc, preferred_element_type=F32)) / l
                o_ref[0, qrows, sl] = heads_merge(o).astype(BF16)


NA_PAIRS = 2


def na_attention(q, k, v, bias):
    B, Tt, _ = q.shape
    L = CTX_LEN
    rows = (Tt - L) // GRID_W
    w = NA_PAIRS * LANES
    return pl.pallas_call(
        functools.partial(_na_kernel, L=L, rows=rows),
        grid=(B, NA_W // w, Tt // TM),
        in_specs=[pl.BlockSpec((1, TM, w), lambda b, j, t: (b, t, j)),
                  pl.BlockSpec((1, Tt, w), lambda b, j, t: (b, 0, j)),
                  pl.BlockSpec((1, Tt, w), lambda b, j, t: (b, 0, j)),
                  pl.BlockSpec((NA_PAIRS,) + bias.shape[1:], lambda b, j, t: (j, 0, 0, 0))],
        out_specs=pl.BlockSpec((1, TM, w), lambda b, j, t: (b, t, j)),
        out_shape=jax.ShapeDtypeStruct((B, Tt, NA_W), BF16),
        compiler_params=_cparams(3), name="na_attention",
    )(q, k, v, bias)


def na_bias_tables(rpb):
    depth, H = rpb.shape[:2]
    cq = jnp.arange(GRID_W)
    col_start = jnp.clip(cq - NA_KW // 2, 0, GRID_W - NA_KW)
    col_mask = (cq[None, :] >= col_start[:, None]) & (cq[None, :] < col_start[:, None] + NA_KW)
    col_idx = jnp.clip(cq[None, :] - cq[:, None] + NA_KW - 1, 0, 2 * NA_KW - 2)
    pick = (col_idx[None] == jnp.arange(2 * NA_KW - 1)[:, None, None]).astype(F32)
    cols = jnp.einsum('lhrc,cqk->lhrqk', rpb, pick, precision=lax.Precision.HIGHEST)
    cols = jnp.where(col_mask, cols, NEG)
    tbl = jnp.stack([cols[:, :, NA_KH - 1 - dl:2 * NA_KH - 1 - dl] for dl in range(NA_KH)], axis=2)
    tbl = jnp.transpose(tbl, (0, 1, 2, 4, 3, 5)).reshape(depth, H // 2, 2, NA_KH, GRID_W, NA_KH * GRID_W)
    return jnp.transpose(tbl, (0, 1, 3, 2, 4, 5)).reshape(depth, H // 2, NA_KH, 2 * GRID_W, NA_KH * GRID_W)


def _log_sigmoid(x):
    return jnp.minimum(x, 0.0) - jnp.log(1.0 + jnp.exp(-jnp.abs(x)))


def _mlstm_chain_group(d, q_ref, k_ref, v_ref, g_ref, o_ref, ct_ref, m_ref, bb):
    n, H = M_CH, M_HEADS
    lane8 = lax.broadcasted_iota(jnp.int32, (2 * H, n), 1)
    row8 = lax.broadcasted_iota(jnp.int32, (2 * H, n), 0)
    lane = lax.broadcasted_iota(jnp.int32, (n, n), 1)
    sub = lax.broadcasted_iota(jnp.int32, (n, n), 0)
    eye = lane == sub
    causal = (lane <= sub) if d == 0 else (lane >= sub)
    one0 = jnp.where(lane == 0, 1.0, 0.0).astype(BF16)

    def scan(x, op, fill):
        sh = 1
        while sh < n:
            if d == 0:
                x = op(x, jnp.where(lane8 >= sh, pltpu.roll(x, sh, 1), fill))
            else:
                x = op(x, jnp.where(lane8 < n - sh, pltpu.roll(x, n - sh, 1), fill))
            sh *= 2
        return x

    g8 = g_ref[bb, d * 2 * H:(d + 1) * 2 * H, :]
    b8 = scan(_log_sigmoid(g8), jnp.add, 0.0)
    last = n - 1 if d == 0 else 0
    bs = pltpu.roll(b8, H, 0)
    tot = jnp.broadcast_to(bs[:, last:last + 1], (2 * H, n))
    r8 = g8 - bs
    m0 = m_ref[d, bb]
    nm8 = jnp.maximum(m0, scan(r8, jnp.maximum, NEG))
    a8 = tot + r8
    m_new = jnp.maximum(tot + m0, jnp.max(a8, axis=1, keepdims=True))
    decay8 = jnp.exp(tot + m0 - m_new)
    w8 = jnp.exp(a8 - m_new)
    e8 = jnp.exp(m0 - nm8)
    t8 = jnp.where(row8 < H, bs, pltpu.roll(nm8, H, 0))
    tc = jnp.concatenate([t8, jnp.zeros((n - 2 * H, n), F32)], axis=0).T
    m_ref[d, bb] = jnp.where(row8 < H, m_new, 0.0)

    for hh in range(H):
        sl = slice(hh * M_HEAD_DIM, (hh + 1) * M_HEAD_DIM)
        qh, kh, vh = q_ref[bb, :, sl], k_ref[bb, :, sl], v_ref[bb, :, sl]
        b_col, nm_col = tc[:, hh:hh + 1], tc[:, H + hh:H + hh + 1]
        dmat = jnp.where(causal, jnp.exp(r8[hh:hh + 1, :] - nm_col), 0.0)
        s = (lax.dot_general(qh, kh, NT_DIMS, preferred_element_type=F32) * dmat).astype(BF16)
        e_diag = jnp.where(eye, e8[hh:hh + 1, :], 0.0).astype(BF16)
        w_diag = jnp.where(eye, w8[hh:hh + 1, :], 0.0).astype(BF16)
        vaug = jnp.concatenate([vh, one0], axis=1)
        ct = ct_ref[d, bb, hh]
        inter = jnp.dot(qh, ct.astype(BF16), preferred_element_type=F32).astype(BF16)
        num = jnp.dot(jnp.concatenate([s, e_diag], axis=1), jnp.concatenate([vaug, inter], axis=0),
                      preferred_element_type=F32)
        den = num[:, M_HEAD_DIM:M_HEAD_DIM + 1]
        o_ref[bb, :, sl] = num[:, :M_HEAD_DIM] / jnp.maximum(jnp.abs(den), jnp.exp(-(b_col + nm_col)))
        wv = jnp.dot(w_diag, vaug, preferred_element_type=F32).astype(BF16)
        ct_ref[d, bb, hh] = decay8[hh:hh + 1, 0:1] * ct + lax.dot_general(kh, wv, TN_DIMS, preferred_element_type=F32)


def _mlstm_kernel(qf_ref, kf_ref, vf_ref, gf_ref, qb_ref, kb_ref, vb_ref, gb_ref, of_ref, ob_ref, ct_ref, m_ref):
    @pl.when(pl.program_id(1) == 0)
    def _():
        ct_ref[...] = jnp.zeros_like(ct_ref)
        m_ref[...] = jnp.zeros_like(m_ref)

    for bb in range(qf_ref.shape[0]):
        _mlstm_chain_group(0, qf_ref, kf_ref, vf_ref, gf_ref, of_ref, ct_ref, m_ref, bb)
        _mlstm_chain_group(1, qb_ref, kb_ref, vb_ref, gb_ref, ob_ref, ct_ref, m_ref, bb)


M_BATCH = 2


def mlstm(q, k, v, gt):
    B, Tt, _ = q.shape
    nc = Tt // M_CH
    lc = CTX_LEN // M_CH
    nb = M_BATCH if B % M_BATCH == 0 else 1

    def bw(s):
        return jnp.where(s < lc, lc - 1 - s, nc - 1 + lc - s)

    tok_f = pl.BlockSpec((nb, M_CH, M_W), lambda b, s: (b, s, 0))
    tok_b = pl.BlockSpec((nb, M_CH, M_W), lambda b, s: (b, bw(s), 0))
    g_f = pl.BlockSpec((nb, N_GATE_COLS, M_CH), lambda b, s: (b, 0, s))
    g_b = pl.BlockSpec((nb, N_GATE_COLS, M_CH), lambda b, s: (b, 0, bw(s)))
    return pl.pallas_call(
        _mlstm_kernel,
        grid=(B // nb, nc),
        in_specs=[tok_f, tok_f, tok_f, g_f, tok_b, tok_b, tok_b, g_b],
        out_specs=[tok_f, tok_b],
        out_shape=[jax.ShapeDtypeStruct((B, Tt, M_W), F32)] * 2,
        scratch_shapes=[pltpu.VMEM((2, nb, M_HEADS, M_HEAD_DIM, M_HEAD_DIM + LANES), F32),
                        pltpu.VMEM((2, nb, 2 * M_HEADS, M_CH), F32)],
        compiler_params=_cparams(2), name="mlstm",
    )(q, k, v, gt, q, k, v, gt)


def _conv_kernel(cur_ref, prev_ref, next_ref, w_ref, b_ref, g_ref, be_ref, o_ref, hbuf, hsh, *, nt):
    t = pl.program_id(1)
    C = CONV_CH

    def glu(u):
        u = u.astype(F32)
        return u[:, :C] * _sigmoid(u[:, C:])

    prev_ok = t >= 2
    next_ok = jnp.logical_and(t >= 1, t < nt - 1)
    hbuf[0:CONV_HALO] = jnp.where(prev_ok, glu(prev_ref[0]), 0.0)
    hbuf[CONV_HALO:CONV_HALO + TM] = glu(cur_ref[0])
    hbuf[CONV_HALO + TM:2 * CONV_HALO + TM] = jnp.where(next_ok, glu(next_ref[0]), 0.0)
    span = TM + 2 * CONV_HALO - 8
    for r in range(8):
        hsh[r] = hbuf[r:r + span, :]
    off = CONV_HALO - CONV_K // 2
    for rc in range(TM // CONV_ROWS):
        acc = jnp.zeros((CONV_ROWS, C), F32) + b_ref[...]
        for j in range(CONV_K):
            a, r = divmod(off + j, 8)
            acc = acc + hsh[r, rc * CONV_ROWS + 8 * a:rc * CONV_ROWS + 8 * a + CONV_ROWS, :] * w_ref[j:j + 1, :]
        mu = jnp.mean(acc, axis=-1, keepdims=True)
        xc = acc - mu
        var = jnp.mean(xc * xc, axis=-1, keepdims=True)
        y = xc * lax.rsqrt(var + NORM_EPS) * g_ref[...] + be_ref[...]
        o_ref[0, rc * CONV_ROWS:(rc + 1) * CONV_ROWS, :] = (y * _sigmoid(y)).astype(BF16)


def conv_module(cu, conv_w, conv_b, ln_g, ln_b):
    B, Tt, W = cu.shape
    nt = Tt // TM
    hb = TM // CONV_HALO
    nhb = Tt // CONV_HALO
    C = CONV_CH
    wpad = jnp.zeros((32, C), F32).at[:CONV_K].set(conv_w)
    vec = lambda a: a.reshape(1, C)
    const = lambda shape: pl.BlockSpec(shape, lambda b, t: (0,) * len(shape))
    return pl.pallas_call(
        functools.partial(_conv_kernel, nt=nt),
        grid=(B, nt),
        in_specs=[pl.BlockSpec((1, TM, W), lambda b, t: (b, t, 0)),
                  pl.BlockSpec((1, CONV_HALO, W), lambda b, t: (b, jnp.maximum(t * hb - 1, 0), 0)),
                  pl.BlockSpec((1, CONV_HALO, W), lambda b, t: (b, jnp.minimum((t + 1) * hb, nhb - 1), 0)),
                  const((32, C)), const((1, C)), const((1, C)), const((1, C))],
        out_specs=pl.BlockSpec((1, TM, C), lambda b, t: (b, t, 0)),
        out_shape=jax.ShapeDtypeStruct((B, Tt, C), BF16),
        scratch_shapes=[pltpu.VMEM((TM + 2 * CONV_HALO, C), F32), pltpu.VMEM((8, TM + 2 * CONV_HALO - 8, C), F32)],
        compiler_params=_cparams(2), name="conv_module",
    )(cu, cu, cu, wpad, vec(conv_b), vec(ln_g), vec(ln_b))


def _merge_kernel(na_ref, hfw_ref, hbw_ref, mo_ref, cv_ref, bg_ref, x_ref, mod_ref, ng_ref,
                  wna_ref, wm_ref, wc_ref, wo_ref, wr_ref, x1_ref, h2_ref, aff_ref, *, D):
    mod = mod_ref[0, 0]
    mh = hfw_ref[0] + hbw_ref[0]
    parts = []
    for hh in range(M_HEADS):
        xh = mh[:, hh * M_HEAD_DIM:(hh + 1) * M_HEAD_DIM]
        parts.append(xh * lax.rsqrt(jnp.mean(xh * xh, axis=-1, keepdims=True) + NORM_EPS))
    mn = (jnp.concatenate(parts, axis=1) * ng_ref[...] * _sigmoid(mo_ref[0].astype(F32))).astype(BF16)
    z = (_sigmoid(bg_ref[0, :, 0:D].astype(F32)) * jnp.dot(na_ref[0], wna_ref[...], preferred_element_type=F32)
         + _sigmoid(bg_ref[0, :, D:2 * D].astype(F32)) * jnp.dot(mn, wm_ref[...], preferred_element_type=F32)
         + _sigmoid(bg_ref[0, :, 2 * D:3 * D].astype(F32)) * jnp.dot(cv_ref[0], wc_ref[...], preferred_element_type=F32))
    y = jnp.dot(z.astype(BF16), wo_ref[...], preferred_element_type=F32)
    x1 = x_ref[0] + mod[:, 2 * D:3 * D] * y
    x1_ref[0] = x1
    h2 = _modnorm(x1, mod[:, 3 * D:4 * D], mod[:, 4 * D:5 * D]).astype(BF16)
    h2_ref[0] = h2
    logits = lax.dot_general(wr_ref[...], h2, NT_DIMS, preferred_element_type=F32)
    e = jnp.exp(logits - jnp.max(logits, axis=0, keepdims=True))
    aff_ref[0] = e / jnp.sum(e, axis=0, keepdims=True)


def merge_and_route(na, hfw, hbw, mo, cv, bg, X, modsel, norm_gain, w_na, w_m, w_c, w_o, w_rt):
    B, Tt, D = X.shape
    E = w_rt.shape[0]
    nt = Tt // TM
    tok = lambda w: pl.BlockSpec((1, TM, w), lambda b, t: (b, t, 0))
    const = lambda shape: pl.BlockSpec(shape, lambda b, t: (0,) * len(shape))
    return pl.pallas_call(
        functools.partial(_merge_kernel, D=D),
        grid=(B, nt),
        in_specs=[tok(NA_W), tok(M_W), tok(M_W), tok(M_W), tok(CONV_CH), tok(3 * D), tok(D),
                  pl.BlockSpec((1, 1, 1, 6 * D), lambda b, t: (b, jnp.minimum(t, 1), 0, 0)),
                  const((1, M_W)), const((NA_W, D)), const((M_W, D)), const((CONV_CH, D)), const((D, D)),
                  const((E, D))],
        out_specs=[tok(D), tok(D), pl.BlockSpec((1, E, TM), lambda b, t: (b, 0, t))],
        out_shape=[jax.ShapeDtypeStruct((B, Tt, D), F32), jax.ShapeDtypeStruct((B, Tt, D), BF16),
                   jax.ShapeDtypeStruct((B, E, Tt), F32)],
        compiler_params=_cparams(2), name="merge_and_route",
    )(na, hfw, hbw, mo, cv, bg, X, modsel, norm_gain, w_na, w_m, w_c, w_o, w_rt)


def _topk_kernel(aff_ref, pos_ref, cnt_ref, *, L, k_ctx, k_lat):
    E, Tt = aff_ref.shape[1:]
    u = pltpu.bitcast(aff_ref[0], jnp.int32)
    is_ctx = lax.broadcasted_iota(jnp.int32, (E, Tt), 1) < L

    def count(mask):
        f = jnp.where(mask, 1.0, 0.0)
        c_ctx = jnp.sum(jnp.where(is_ctx, f, 0.0), axis=1, keepdims=True)
        return c_ctx, jnp.sum(f, axis=1, keepdims=True) - c_ctx

    def bit_step(i, carry):
        v_ctx, v_lat = carry
        bit = jnp.left_shift(jnp.int32(1), 30 - i)
        c_ctx, c_lat = count(u >= jnp.where(is_ctx, v_ctx | bit, v_lat | bit))
        return (jnp.where(c_ctx >= k_ctx, v_ctx | bit, v_ctx), jnp.where(c_lat >= k_lat, v_lat | bit, v_lat))

    zero = jnp.zeros((E, 1), jnp.int32)
    v_ctx, v_lat = lax.fori_loop(0, 31, bit_step, (zero, zero))
    thr = jnp.where(is_ctx, v_ctx, v_lat)
    g_ctx, g_lat = count(u > thr)
    need_ctx, need_lat = k_ctx - g_ctx, k_lat - g_lat

    r = lax.broadcasted_iota(jnp.int32, (TM, TM), 0)
    c = lax.broadcasted_iota(jnp.int32, (TM, TM), 1)
    tri = jnp.where(r < c, 1.0, 0.0).astype(BF16)
    lane_t = lax.broadcasted_iota(jnp.int32, (E, LANES), 1)
    cnt = jnp.zeros((E, LANES), F32)
    ties = jnp.zeros((E, 1), F32)
    slots = jnp.zeros((E, 1), F32)
    nt = Tt // TM
    for t in range(nt):
        if t == 1:
            ties = jnp.zeros((E, 1), F32)
        need = need_ctx if t == 0 else need_lat
        ut = pltpu.bitcast(aff_ref[0, :, t * TM:(t + 1) * TM], jnp.int32)
        tt = v_ctx if t == 0 else v_lat
        eq = jnp.where(ut == tt, 1.0, 0.0)
        rank = ties + jnp.dot(eq.astype(BF16), tri, preferred_element_type=F32)
        sel = jnp.where(ut > tt, 1.0, jnp.where(rank < need, eq, 0.0))
        pos = slots + jnp.dot(sel.astype(BF16), tri, preferred_element_type=F32)
        pos_ref[0, :, t * TM:(t + 1) * TM] = jnp.where(sel > 0.0, pos, -1.0).astype(jnp.int32)
        cnt = jnp.where(lane_t == t, slots, cnt)
        ties = ties + jnp.sum(eq, axis=1, keepdims=True)
        slots = slots + jnp.sum(sel, axis=1, keepdims=True)
    cnt = jnp.where(lane_t == nt, slots, cnt)
    cnt_ref[0] = cnt.astype(jnp.int32)


def expert_choice_slots(aff):
    B, E, Tt = aff.shape
    L = CTX_LEN
    k_ctx = EC_FACTOR * L // N_EXPERTS
    k_lat = EC_FACTOR * (Tt - L) // N_EXPERTS
    assert Tt // TM + 1 <= LANES
    return pl.pallas_call(
        functools.partial(_topk_kernel, L=L, k_ctx=k_ctx, k_lat=k_lat),
        grid=(B,),
        in_specs=[pl.BlockSpec((1, E, Tt), lambda b: (b, 0, 0))],
        out_specs=[pl.BlockSpec((1, E, Tt), lambda b: (b, 0, 0)), pl.BlockSpec((1, E, LANES), lambda b: (b, 0, 0))],
        out_shape=[jax.ShapeDtypeStruct((B, E, Tt), jnp.int32), jax.ShapeDtypeStruct((B, E, LANES), jnp.int32)],
        compiler_params=_cparams(1), name="expert_choice_slots",
    )(aff)


def _ffn_kernel(cnt_ref, h_ref, pos_ref, wg_ref, wu_ref, wd_ref, y_ref, xin_ref, acc_ref, *, ns, nt):
    b, e, f = pl.program_id(0), pl.program_id(1), pl.program_id(2)
    W = SLOT_WIN

    @pl.when(f == 0)
    def _gather():
        xin_ref[...] = jnp.zeros_like(xin_ref)
        acc_ref[...] = jnp.zeros_like(acc_ref)

        def tile_body(t, carry):
            lo, hi = cnt_ref[b, e, t], cnt_ref[b, e, t + 1]
            tok0 = pl.multiple_of(t * TM, TM)
            s0 = (lo // 16) * 16

            def window(r, c2):
                base = pl.multiple_of(s0 + r * W, 16)
                srow = base + lax.broadcasted_iota(jnp.int32, (W, TM), 0)
                onehot = jnp.where(pos_ref[0, 0, :, pl.ds(tok0, TM)] == srow, 1.0, 0.0).astype(BF16)
                rows = jnp.dot(onehot, h_ref[0, pl.ds(tok0, TM), :], preferred_element_type=F32)
                xin_ref[pl.ds(base, W), :] = (xin_ref[pl.ds(base, W), :].astype(F32) + rows).astype(BF16)
                return c2

            lax.fori_loop(0, jnp.where(hi > lo, (hi - s0 + W - 1) // W, 0), window, 0)
            return carry

        lax.fori_loop(0, nt, tile_body, 0)

    x = xin_ref[0:ns, :]
    a = jnp.dot(x, wg_ref[0], preferred_element_type=F32)
    u = jnp.dot(x, wu_ref[0], preferred_element_type=F32)
    hmid = (a * _sigmoid(a) * u).astype(BF16)
    acc_ref[...] += jnp.dot(hmid, wd_ref[0], preferred_element_type=F32)

    @pl.when(f == pl.num_programs(2) - 1)
    def _():
        y_ref[0, 0, 0:ns, :] = acc_ref[...].astype(BF16)
        y_ref[0, 0, ns:ns + SLOT_WIN, :] = jnp.zeros((SLOT_WIN, y_ref.shape[-1]), BF16)


def expert_ffn(cnt, h2, pos, w_gate, w_up, w_down):
    B, Tt, D = h2.shape
    E, _, F = w_gate.shape
    L = CTX_LEN
    ns = EC_FACTOR * L // N_EXPERTS + EC_FACTOR * (Tt - L) // N_EXPERTS
    assert ns % 16 == 0
    tf = min(512, F)
    grid_spec = pltpu.PrefetchScalarGridSpec(
        num_scalar_prefetch=1,
        grid=(B, E, F // tf),
        in_specs=[pl.BlockSpec((1, Tt, D), lambda b, e, f, c: (b, 0, 0), pipeline_mode=pl.Buffered(1)),
                  pl.BlockSpec((1, 1, 1, Tt), lambda b, e, f, c: (b, e, 0, 0)),
                  pl.BlockSpec((1, D, tf), lambda b, e, f, c: (e, 0, f)),
                  pl.BlockSpec((1, D, tf), lambda b, e, f, c: (e, 0, f)),
                  pl.BlockSpec((1, tf, D), lambda b, e, f, c: (e, f, 0))],
        out_specs=pl.BlockSpec((1, 1, ns + SLOT_WIN, D), lambda b, e, f, c: (b, e, 0, 0)),
        scratch_shapes=[pltpu.VMEM((ns + SLOT_WIN, D), BF16), pltpu.VMEM((ns, D), F32)])
    return pl.pallas_call(
        functools.partial(_ffn_kernel, ns=ns, nt=Tt // TM),
        grid_spec=grid_spec,
        out_shape=jax.ShapeDtypeStruct((B, E, ns + SLOT_WIN, D), BF16),
        compiler_params=_cparams(3), name="expert_ffn",
    )(cnt, h2, pos.reshape(B, E, 1, Tt), w_gate, w_up, w_down)


def _combine_kernel(cnt_ref, pos_ref, aff_ref, y_ref, x_ref, mod_ref, o_ref, yw_ref, p_ref, acc_ref, *, D, ns):
    b, t = pl.program_id(0), pl.program_id(1)
    E = pos_ref.shape[1]
    W = SLOT_WIN
    starts, rounds = [], jnp.int32(0)
    for e in range(E):
        lo, hi = cnt_ref[b, e, t], cnt_ref[b, e, t + 1]
        s0 = (lo // 16) * 16
        starts.append(s0)
        rounds = jnp.maximum(rounds, jnp.where(hi > lo, (hi - s0 + W - 1) // W, 0))
    acc_ref[...] = jnp.zeros_like(acc_ref)

    def round_body(r, carry):
        for e in range(E):
            base = pl.multiple_of(jnp.minimum(starts[e] + r * W, ns), 16)
            yw_ref[e * W:(e + 1) * W, :] = y_ref[0, e, pl.ds(base, W), :]
            srow = base + lax.broadcasted_iota(jnp.int32, (W, TM), 0)
            p_ref[e * W:(e + 1) * W, :] = jnp.where(pos_ref[0, e:e + 1, :] == srow, aff_ref[0, e:e + 1, :], 0.0).astype(BF16)
        acc_ref[...] += lax.dot_general(p_ref[...], yw_ref[...], TN_DIMS, preferred_element_type=F32)
        return carry

    lax.fori_loop(0, rounds, round_body, 0)
    o_ref[0] = x_ref[0] + mod_ref[0, 0][:, 5 * D:6 * D] * acc_ref[...]


def combine(cnt, pos, aff, Y, X1, modsel):
    B, Tt, D = X1.shape
    E = pos.shape[1]
    nsp = Y.shape[2]
    grid_spec = pltpu.PrefetchScalarGridSpec(
        num_scalar_prefetch=1,
        grid=(B, Tt // TM),
        in_specs=[pl.BlockSpec((1, E, TM), lambda b, t, c: (b, 0, t)),
                  pl.BlockSpec((1, E, TM), lambda b, t, c: (b, 0, t)),
                  pl.BlockSpec((1, E, nsp, D), lambda b, t, c: (b, 0, 0, 0), pipeline_mode=pl.Buffered(1)),
                  pl.BlockSpec((1, TM, D), lambda b, t, c: (b, t, 0)),
                  pl.BlockSpec((1, 1, 1, 6 * D), lambda b, t, c: (b, jnp.minimum(t, 1), 0, 0))],
        out_specs=pl.BlockSpec((1, TM, D), lambda b, t, c: (b, t, 0)),
        scratch_shapes=[pltpu.VMEM((E * SLOT_WIN, D), BF16), pltpu.VMEM((E * SLOT_WIN, TM), BF16),
                        pltpu.VMEM((TM, D), F32)])
    return pl.pallas_call(
        functools.partial(_combine_kernel, D=D, ns=nsp - SLOT_WIN),
        grid_spec=grid_spec,
        out_shape=jax.ShapeDtypeStruct((B, Tt, D), F32),
        compiler_params=_cparams(2), name="combine",
    )(cnt, pos, aff, Y, X1, modsel)


def _pack_w_in(w, D):
    widths = (NA_W, NA_W, NA_W, M_W, M_W, M_W, M_W, N_GATE_COLS, 2 * CONV_CH, 3 * D)
    offs = [0]
    for wd in widths:
        offs.append(offs[-1] + wd)
    seg = [w[:, offs[i]:offs[i + 1]] for i in range(len(widths))]
    return jnp.concatenate(seg[:7] + seg[8:], axis=1).astype(BF16), seg[7].T.astype(BF16)


def _rope_tables(T, L):
    nf = M_HEAD_DIM // 4
    t = jnp.arange(T)
    row = (t // GRID_W).astype(F32)
    col = (t % GRID_W).astype(F32)
    inv = ROPE_BASE ** (-jnp.arange(nf, dtype=F32) / nf)
    ar, ac = row[:, None] * inv[None, :], col[:, None] * inv[None, :]
    z = jnp.zeros((T, nf), F32)
    cos = jnp.concatenate([jnp.cos(ar), jnp.cos(ar), jnp.cos(ac), jnp.cos(ac)], axis=1)
    sina = jnp.concatenate([-jnp.sin(ar), z, -jnp.sin(ac), z], axis=1)
    sinb = jnp.concatenate([z, jnp.sin(ar), z, jnp.sin(ac)], axis=1)
    ident = lambda v: jnp.full((L, M_HEAD_DIM), v, F32)
    return (jnp.concatenate([ident(1.0), cos]), jnp.concatenate([ident(0.0), sina]),
            jnp.concatenate([ident(0.0), sinb]))


def kernel(x, c, ctx, c_ctx, w_mod, b_mod, w_in, na_q_gain, na_k_gain, na_rpb, m_gate_bias,
           m_norm_gain, conv_w, conv_b, conv_ln_g, conv_ln_b, w_br_na, w_br_m, w_br_conv, w_out,
           w_router, w_gate, w_up, w_down):
    B, T, D = x.shape
    L = ctx.shape[1]
    assert L == CTX_LEN == TM and T % TM == 0 and B + 1 <= 16
    depth = w_in.shape[0]
    X = jnp.concatenate([ctx, x], axis=1)
    cc = jnp.zeros((16, D), F32).at[:B].set(c).at[B].set(c_ctx)
    mods = modulation(cc, w_mod, b_mod)
    cos, sina, sinb = _rope_tables(T, L)
    hd = jnp.arange(NA_W) // NA_HEAD_DIM
    gmat = (hd[:, None] == hd[None, :]).astype(BF16)
    bias_tables = na_bias_tables(na_rpb)
    for l in range(depth):
        modsel = jnp.stack([jnp.broadcast_to(mods[l, B], (B, 6 * D)), mods[l, :B]], axis=1).reshape(B, 2, 1, 6 * D)
        q_gain = (jnp.tile(na_q_gain[l], NA_HEADS) * NA_HEAD_DIM ** -0.5).reshape(1, NA_W)
        k_gain = jnp.tile(na_k_gain[l], NA_HEADS).reshape(1, NA_W)
        w_tok, w_gates_t = _pack_w_in(w_in[l], D)
        naq, nak, nav, mq, mk, mv, mo, mg, cu, bg = input_projection(
            X, modsel, w_tok, q_gain, k_gain, gmat, cos, sina, sinb, w_gates_t, m_gate_bias[l].reshape(N_GATE_COLS, 1))
        na = na_attention(naq, nak, nav, bias_tables[l])
        hfw, hbw = mlstm(mq, mk, mv, mg)
        cv = conv_module(cu, conv_w[l], conv_b[l], conv_ln_g[l], conv_ln_b[l])
        X1, h2, aff = merge_and_route(
            na, hfw, hbw, mo, cv, bg, X, modsel, m_norm_gain[l].reshape(1, M_W),
            w_br_na[l].astype(BF16), w_br_m[l].astype(BF16), w_br_conv[l].astype(BF16), w_out[l].astype(BF16),
            w_router[l].T.astype(BF16))
        pos, cnt = expert_choice_slots(aff)
        cnt = cnt[:, :, :T // TM + 2]
        Y = expert_ffn(cnt, h2, pos, w_gate[l].astype(BF16), w_up[l].astype(BF16), w_down[l].astype(BF16))
        X = combine(cnt, pos, aff, Y, X1, modsel)
    return X[:, L:, :]
```

```python
import functools

import jax
import jax.numpy as jnp
from jax import lax
from jax.experimental import pallas as pl
from jax.experimental.pallas import tpu as pltpu

DEPTH = 4
CTX_LEN = 256
GRID_W = 64
NA_HEADS = 8
NA_HEAD_DIM = 64
NA_KH = 8
NA_KW = 16
M_HEADS = 4
M_HEAD_DIM = 128
CONV_CH = 512
CONV_K = 31
N_EXPERTS = 16
EC_FACTOR = 2
ROPE_BASE = 10000.0
NORM_EPS = 1e-6

NA_W = NA_HEADS * NA_HEAD_DIM
M_W = M_HEADS * M_HEAD_DIM
N_GATE_COLS = 4 * M_HEADS

LANES = 128
TM = 256
M_CH = 128
CONV_HALO = 16
CONV_ROWS = 64
SLOT_WIN = 64
NEG = -1e30
VMEM_LIMIT_BYTES = 56 * 1024 * 1024

F32 = jnp.float32
BF16 = jnp.bfloat16
NT_DIMS = (((1,), (1,)), ((), ()))
TN_DIMS = (((0,), (0,)), ((), ()))


def _cparams(n_axes):
    return pltpu.CompilerParams(dimension_semantics=("arbitrary",) * n_axes,
                                vmem_limit_bytes=VMEM_LIMIT_BYTES)


def _sigmoid(x):
    return 1.0 / (1.0 + jnp.exp(-x))


def _modnorm(x, shift, scale):
    ms = jnp.mean(x * x, axis=-1, keepdims=True)
    return (x * lax.rsqrt(ms + NORM_EPS)) * (1.0 + scale) + shift


def _chunk(width):
    return 512 if width % 512 == 0 else 256


def _mod_kernel(c_ref, w_ref, b_ref, o_ref):
    c = c_ref[...]
    s = (c * _sigmoid(c)).astype(BF16)
    o_ref[0] = jnp.dot(s, w_ref[0].astype(BF16), preferred_element_type=F32) + b_ref[0]


def modulation(cc, w_mod, b_mod):
    depth, D, N = w_mod.shape
    tn = 1024 if N % 1024 == 0 else N
    return pl.pallas_call(
        _mod_kernel,
        grid=(depth, N // tn),
        in_specs=[pl.BlockSpec((16, D), lambda l, n: (0, 0)),
                  pl.BlockSpec((1, D, tn), lambda l, n: (l, 0, n)),
                  pl.BlockSpec((1, 1, tn), lambda l, n: (l, 0, n))],
        out_specs=pl.BlockSpec((1, 16, tn), lambda l, n: (l, 0, n)),
        out_shape=jax.ShapeDtypeStruct((depth, 16, N), F32),
        compiler_params=_cparams(2), name="modulation",
    )(cc, w_mod, b_mod.reshape(depth, 1, N))


def _inproj_kernel(x_ref, mod_ref, w_ref, qg_ref, kg_ref, gm_ref, cos_ref, sa_ref, sb_ref, wgt_ref, gb_ref,
                   naq_ref, nak_ref, nav_ref, mq_ref, mk_ref, mv_ref, mo_ref, mgf_ref, mgb_ref, cu_ref, bg_ref, *, D):
    mod = mod_ref[0, 0]
    h = _modnorm(x_ref[0], mod[:, 0:D], mod[:, D:2 * D]).astype(BF16)

    def proj(off, width):
        return jnp.dot(h, w_ref[:, off:off + width], preferred_element_type=F32)

    gm = gm_ref[...]
    for ref, gain_ref, off in ((naq_ref, qg_ref, 0), (nak_ref, kg_ref, NA_W)):
        y = proj(off, NA_W)
        ysq = y * y
        hi = ysq.astype(BF16)
        lo = (ysq - hi.astype(F32)).astype(BF16)
        ss = jnp.dot(hi, gm, preferred_element_type=F32) + jnp.dot(lo, gm, preferred_element_type=F32)
        ref[0] = (y * lax.rsqrt(ss * (1.0 / NA_HEAD_DIM) + NORM_EPS) * gain_ref[...]).astype(BF16)
    nav_ref[0] = proj(2 * NA_W, NA_W).astype(BF16)

    cos, sa, sb = cos_ref[...], sa_ref[...], sb_ref[...]
    q4 = M_HEAD_DIM // 4
    base = 3 * NA_W
    for ref, off, scale in ((mq_ref, base, 1.0), (mk_ref, base + M_W, M_HEAD_DIM ** -0.5)):
        y = proj(off, M_W)
        for hh in range(M_HEADS):
            yh = y[:, hh * M_HEAD_DIM:(hh + 1) * M_HEAD_DIM]
            r = yh * cos + pltpu.roll(yh, M_HEAD_DIM - q4, 1) * sa + pltpu.roll(yh, q4, 1) * sb
            ref[0, :, hh * M_HEAD_DIM:(hh + 1) * M_HEAD_DIM] = (r * scale).astype(BF16)
    mv_ref[0] = proj(base + 2 * M_W, M_W).astype(BF16)
    mo_ref[0] = proj(base + 3 * M_W, M_W).astype(BF16)
    base += 4 * M_W
    for ref, width in ((cu_ref, 2 * CONV_CH), (bg_ref, 3 * D)):
        cw = _chunk(width)
        for j in range(width // cw):
            ref[0, :, j * cw:(j + 1) * cw] = proj(base + j * cw, cw).astype(BF16)
        base += width
    g = lax.dot_general(wgt_ref[...], h, NT_DIMS, preferred_element_type=F32) + gb_ref[...]
    nrow = g.shape[0] // 2
    for cidx in range(g.shape[1] // M_CH):
        mgf_ref[0, cidx] = g[0:nrow, cidx * M_CH:(cidx + 1) * M_CH]
        mgb_ref[0, cidx] = g[nrow:2 * nrow, cidx * M_CH:(cidx + 1) * M_CH]


def input_projection(X, modsel, w_packed, q_gain, k_gain, gmat, cos, sina, sinb, w_gates_t, gate_bias):
    B, Tt, D = X.shape
    NP = w_packed.shape[1]
    nt = Tt // TM
    tok = lambda w: pl.BlockSpec((1, TM, w), lambda b, t: (b, t, 0))
    const = lambda shape: pl.BlockSpec(shape, lambda b, t: (0,) * len(shape))
    out_widths = (NA_W, NA_W, NA_W, M_W, M_W, M_W, M_W, None, None, 2 * CONV_CH, 3 * D)
    gate_rows = w_gates_t.shape[0] // 2
    gate_spec = pl.BlockSpec((1, TM // M_CH, gate_rows, M_CH), lambda b, t: (b, t, 0, 0))
    return pl.pallas_call(
        functools.partial(_inproj_kernel, D=D),
        grid=(B, nt),
        in_specs=[tok(D),
                  pl.BlockSpec((1, 1, 1, 6 * D), lambda b, t: (b, jnp.minimum(t, 1), 0, 0)),
                  pl.BlockSpec((D, NP), lambda b, t: (0, 0), pipeline_mode=pl.Buffered(1)),
                  const((1, NA_W)), const((1, NA_W)), const((NA_W, NA_W)),
                  pl.BlockSpec((TM, M_HEAD_DIM), lambda b, t: (t, 0)),
                  pl.BlockSpec((TM, M_HEAD_DIM), lambda b, t: (t, 0)),
                  pl.BlockSpec((TM, M_HEAD_DIM), lambda b, t: (t, 0)),
                  const((2 * gate_rows, D)), const((2 * gate_rows, 1))],
        out_specs=[gate_spec if w is None else tok(w) for w in out_widths],
        out_shape=[jax.ShapeDtypeStruct((B, Tt // M_CH, gate_rows, M_CH), F32) if w is None
                   else jax.ShapeDtypeStruct((B, Tt, w), BF16) for w in out_widths],
        compiler_params=_cparams(2), name="input_projection",
    )(X, modsel, w_packed, q_gain, k_gain, gmat, cos, sina, sinb, w_gates_t, gate_bias)


def _na_kernel(q_ref, k_ref, v_ref, bias_ref, o_ref, *, L, rows):
    t = pl.program_id(2)
    even = lax.broadcasted_iota(jnp.int32, (1, LANES), 1) < NA_HEAD_DIM
    pairs = [slice(jj * LANES, (jj + 1) * LANES) for jj in range(NA_PAIRS)]

    def heads_stack(q):
        z = jnp.zeros_like(q)
        return jnp.concatenate([jnp.where(even, q, z), jnp.where(even, z, q)], axis=0)

    def heads_merge(o):
        n = o.shape[0] // 2
        return jnp.where(even, o[:n], o[n:])

    @pl.when(t == 0)
    def _context_queries():
        for sl in pairs:
            kc, vc = k_ref[0, 0:L, sl], v_ref[0, 0:L, sl]
            q2 = heads_stack(q_ref[0, :, sl])
            s = lax.dot_general(q2, kc, NT_DIMS, preferred_element_type=F32)
            m = jnp.max(s, axis=-1, keepdims=True)
            p = jnp.exp(s - m)
            l = jnp.sum(p, axis=-1, keepdims=True)
            o = jnp.dot(p.astype(BF16), vc, preferred_element_type=F32) / l
            o_ref[0, :, sl] = heads_merge(o).astype(BF16)

    @pl.when(t > 0)
    def _latent_queries():
        for rr in range(TM // GRID_W):
            r = (t - 1) * (TM // GRID_W) + rr
            rs = jnp.clip(r - NA_KH // 2, 0, rows - NA_KH)
            start = pl.multiple_of(L + rs * GRID_W, GRID_W)
            qrows = slice(rr * GRID_W, (rr + 1) * GRID_W)
            for jj, sl in enumerate(pairs):
                kc, vc = k_ref[0, 0:L, sl], v_ref[0, 0:L, sl]
                kw = k_ref[0, pl.ds(start, NA_KH * GRID_W), sl]
                vw = v_ref[0, pl.ds(start, NA_KH * GRID_W), sl]
                q2 = heads_stack(q_ref[0, qrows, sl])
                s_lat = lax.dot_general(q2, kw, NT_DIMS, preferred_element_type=F32) + bias_ref[jj, r - rs]
                s_ctx = lax.dot_general(q2, kc, NT_DIMS, preferred_element_type=F32)
                m = jnp.maximum(jnp.max(s_lat, axis=-1, keepdims=True), jnp.max(s_ctx, axis=-1, keepdims=True))
                p_lat = jnp.exp(s_lat - m)
                p_ctx = jnp.exp(s_ctx - m)
                l = jnp.sum(p_lat, axis=-1, keepdims=True) + jnp.sum(p_ctx, axis=-1, keepdims=True)
                o = (jnp.dot(p_lat.astype(BF16), vw, preferred_element_type=F32)
                     + jnp.dot(p_ctx.astype(BF16), vc, preferred_element_type=F32)) / l
                o_ref[0, qrows, sl] = heads_merge(o).astype(BF16)


NA_PAIRS = 2


def na_attention(q, k, v, bias):
    B, Tt, _ = q.shape
    L = CTX_LEN
    rows = (Tt - L) // GRID_W
    w = NA_PAIRS * LANES
    return pl.pallas_call(
        functools.partial(_na_kernel, L=L, rows=rows),
        grid=(B, NA_W // w, Tt // TM),
        in_specs=[pl.BlockSpec((1, TM, w), lambda b, j, t: (b, t, j)),
                  pl.BlockSpec((1, Tt, w), lambda b, j, t: (b, 0, j)),
                  pl.BlockSpec((1, Tt, w), lambda b, j, t: (b, 0, j)),
                  pl.BlockSpec((NA_PAIRS,) + bias.shape[1:], lambda b, j, t: (j, 0, 0, 0))],
        out_specs=pl.BlockSpec((1, TM, w), lambda b, j, t: (b, t, j)),
        out_shape=jax.ShapeDtypeStruct((B, Tt, NA_W), BF16),
        compiler_params=_cparams(3), name="na_attention",
    )(q, k, v, bias)


def na_bias_tables(rpb):
    depth, H = rpb.shape[:2]
    cq = jnp.arange(GRID_W)
    col_start = jnp.clip(cq - NA_KW // 2, 0, GRID_W - NA_KW)
    col_mask = (cq[None, :] >= col_start[:, None]) & (cq[None, :] < col_start[:, None] + NA_KW)
    col_idx = jnp.clip(cq[None, :] - cq[:, None] + NA_KW - 1, 0, 2 * NA_KW - 2)
    pick = (col_idx[None] == jnp.arange(2 * NA_KW - 1)[:, None, None]).astype(F32)
    cols = jnp.einsum('lhrc,cqk->lhrqk', rpb, pick, precision=lax.Precision.HIGHEST)
    cols = jnp.where(col_mask, cols, NEG)
    tbl = jnp.stack([cols[:, :, NA_KH - 1 - dl:2 * NA_KH - 1 - dl] for dl in range(NA_KH)], axis=2)
    tbl = jnp.transpose(tbl, (0, 1, 2, 4, 3, 5)).reshape(depth, H // 2, 2, NA_KH, GRID_W, NA_KH * GRID_W)
    return jnp.transpose(tbl, (0, 1, 3, 2, 4, 5)).reshape(depth, H // 2, NA_KH, 2 * GRID_W, NA_KH * GRID_W)


def _log_sigmoid(x):
    return jnp.minimum(x, 0.0) - jnp.log(1.0 + jnp.exp(-jnp.abs(x)))


def _mlstm_gate_kernel(gf_ref, gb_ref, rqf_ref, rqb_ref, col_ref, t_scr):
    nc, n = gf_ref.shape[1], M_CH
    rows = nc * 8
    lane = lax.broadcasted_iota(jnp.int32, (rows, n), 1)
    for d, (g_ref, rq_ref) in enumerate(((gf_ref, rqf_ref), (gb_ref, rqb_ref))):
        def scan(x, op, fill):
            sh = 1
            while sh < n:
                if d == 0:
                    x = op(x, jnp.where(lane >= sh, pltpu.roll(x, sh, 1), fill))
                else:
                    x = op(x, jnp.where(lane < n - sh, pltpu.roll(x, n - sh, 1), fill))
                sh *= 2
            return x

        top = g_ref[0, :, 0:8, :].reshape(rows, n)
        bot = g_ref[0, :, 8:16, :].reshape(rows, n)
        b = scan(_log_sigmoid(bot), jnp.add, 0.0)
        last = n - 1 if d == 0 else 0
        tot = jnp.broadcast_to(b[:, last:last + 1], (rows, n))
        r = top - b
        cm = scan(r, jnp.maximum, NEG)
        amax = jnp.broadcast_to(jnp.max(tot + r, axis=1, keepdims=True), (rows, n))
        for j, val in enumerate((r, cm, tot, amax)):
            rq_ref[0, :, 8 * j:8 * (j + 1), :] = val.reshape(nc, 8, n)
        t_scr[:, 16 * d:16 * d + 8, :] = b.reshape(nc, 8, n)
        t_scr[:, 16 * d + 8:16 * d + 16, :] = cm.reshape(nc, 8, n)

    def transpose_chunk(ci, carry):
        tile = jnp.concatenate([t_scr[ci], jnp.zeros((n - 32, n), F32)], axis=0)
        col_ref[0, ci] = tile.T
        return carry

    lax.fori_loop(0, nc, transpose_chunk, 0)


def mlstm_gates(gf, gb):
    B, nc, _, n = gf.shape
    blk = lambda r: pl.BlockSpec((1, nc, r, n), lambda b: (b, 0, 0, 0))
    return pl.pallas_call(
        _mlstm_gate_kernel,
        grid=(B,),
        in_specs=[blk(16), blk(16)],
        out_specs=[blk(32), blk(32), blk(n)],
        out_shape=[jax.ShapeDtypeStruct((B, nc, 32, n), F32)] * 2 + [jax.ShapeDtypeStruct((B, nc, n, n), F32)],
        scratch_shapes=[pltpu.VMEM((nc, 32, n), F32)],
        compiler_params=_cparams(1), name="mlstm_gates",
    )(gf, gb)


def _mlstm_chain_group(d, q_ref, k_ref, v_ref, rq_ref, col_ref, o_ref, ct_ref, m_ref, bb):
    n, H = M_CH, M_HEADS
    row8 = lax.broadcasted_iota(jnp.int32, (2 * H, n), 0)
    lane = lax.broadcasted_iota(jnp.int32, (n, n), 1)
    sub = lax.broadcasted_iota(jnp.int32, (n, n), 0)
    eye = lane == sub
    causal = (lane <= sub) if d == 0 else (lane >= sub)
    one0 = jnp.where(lane == 0, 1.0, 0.0).astype(BF16)

    r8, cm8 = rq_ref[bb, 0, 0:8, :], rq_ref[bb, 0, 8:16, :]
    tot, amax = rq_ref[bb, 0, 16:24, :], rq_ref[bb, 0, 24:32, :]
    m0 = m_ref[d, bb]
    nm8 = jnp.maximum(m0, cm8)
    m_new = jnp.maximum(tot + m0, amax)
    decay8 = jnp.exp(tot + m0 - m_new)
    w8 = jnp.exp(tot + r8 - m_new)
    e8 = jnp.exp(m0 - nm8)
    m_ref[d, bb] = jnp.where(row8 < H, m_new, 0.0)
    tc = col_ref[bb, 0]

    for hh in range(H):
        sl = slice(hh * M_HEAD_DIM, (hh + 1) * M_HEAD_DIM)
        qh, kh, vh = q_ref[bb, :, sl], k_ref[bb, :, sl], v_ref[bb, :, sl]
        b_col = tc[:, 16 * d + hh:16 * d + hh + 1]
        nm_col = jnp.maximum(m0[hh:hh + 1, 0:1], tc[:, 16 * d + 8 + hh:16 * d + 8 + hh + 1])
        dmat = jnp.where(causal, jnp.exp(r8[hh:hh + 1, :] - nm_col), 0.0)
        s = (lax.dot_general(qh, kh, NT_DIMS, preferred_element_type=F32) * dmat).astype(BF16)
        e_diag = jnp.where(eye, e8[hh:hh + 1, :], 0.0).astype(BF16)
        w_diag = jnp.where(eye, w8[hh:hh + 1, :], 0.0).astype(BF16)
        vaug = jnp.concatenate([vh, one0], axis=1)
        ct = ct_ref[d, bb, hh]
        inter = jnp.dot(qh, ct.astype(BF16), preferred_element_type=F32).astype(BF16)
        num = jnp.dot(jnp.concatenate([s, e_diag], axis=1), jnp.concatenate([vaug, inter], axis=0),
                      preferred_element_type=F32)
        den = num[:, M_HEAD_DIM:M_HEAD_DIM + 1]
        o_ref[bb, :, sl] = num[:, :M_HEAD_DIM] / jnp.maximum(jnp.abs(den), jnp.exp(-(b_col + nm_col)))
        wv = jnp.dot(w_diag, vaug, preferred_element_type=F32).astype(BF16)
        ct_ref[d, bb, hh] = decay8[hh:hh + 1, 0:1] * ct + lax.dot_general(kh, wv, TN_DIMS, preferred_element_type=F32)


def _mlstm_kernel(qf_ref, kf_ref, vf_ref, rqf_ref, colf_ref, qb_ref, kb_ref, vb_ref, rqb_ref, colb_ref,
                  of_ref, ob_ref, ct_ref, m_ref):
    @pl.when(pl.program_id(1) == 0)
    def _():
        ct_ref[...] = jnp.zeros_like(ct_ref)
        m_ref[...] = jnp.zeros_like(m_ref)

    for bb in range(qf_ref.shape[0]):
        _mlstm_chain_group(0, qf_ref, kf_ref, vf_ref, rqf_ref, colf_ref, of_ref, ct_ref, m_ref, bb)
        _mlstm_chain_group(1, qb_ref, kb_ref, vb_ref, rqb_ref, colb_ref, ob_ref, ct_ref, m_ref, bb)


M_BATCH = 2


def mlstm(q, k, v, gf, gb):
    B, Tt, _ = q.shape
    nc = Tt // M_CH
    lc = CTX_LEN // M_CH
    nb = M_BATCH if B % M_BATCH == 0 else 1
    rqf, rqb, col = mlstm_gates(gf, gb)

    def bw(s):
        return jnp.where(s < lc, lc - 1 - s, nc - 1 + lc - s)

    tok_f = pl.BlockSpec((nb, M_CH, M_W), lambda b, s: (b, s, 0))
    tok_b = pl.BlockSpec((nb, M_CH, M_W), lambda b, s: (b, bw(s), 0))
    chunk_f = lambda r: pl.BlockSpec((nb, 1, r, M_CH), lambda b, s: (b, s, 0, 0))
    chunk_b = lambda r: pl.BlockSpec((nb, 1, r, M_CH), lambda b, s: (b, bw(s), 0, 0))
    return pl.pallas_call(
        _mlstm_kernel,
        grid=(B // nb, nc),
        in_specs=[tok_f, tok_f, tok_f, chunk_f(32), chunk_f(M_CH), tok_b, tok_b, tok_b, chunk_b(32), chunk_b(M_CH)],
        out_specs=[tok_f, tok_b],
        out_shape=[jax.ShapeDtypeStruct((B, Tt, M_W), F32)] * 2,
        scratch_shapes=[pltpu.VMEM((2, nb, M_HEADS, M_HEAD_DIM, M_HEAD_DIM + LANES), F32),
                        pltpu.VMEM((2, nb, 2 * M_HEADS, M_CH), F32)],
        compiler_params=_cparams(2), name="mlstm",
    )(q, k, v, rqf, col, q, k, v, rqb, col)


def _conv_kernel(cur_ref, prev_ref, next_ref, w_ref, b_ref, g_ref, be_ref, o_ref, hbuf, hsh, *, nt):
    t = pl.program_id(1)
    C = CONV_CH

    def glu(u):
        u = u.astype(F32)
        return u[:, :C] * _sigmoid(u[:, C:])

    prev_ok = t >= 2
    next_ok = jnp.logical_and(t >= 1, t < nt - 1)
    hbuf[0:CONV_HALO] = jnp.where(prev_ok, glu(prev_ref[0]), 0.0)
    hbuf[CONV_HALO:CONV_HALO + TM] = glu(cur_ref[0])
    hbuf[CONV_HALO + TM:2 * CONV_HALO + TM] = jnp.where(next_ok, glu(next_ref[0]), 0.0)
    span = TM + 2 * CONV_HALO - 8
    for r in range(8):
        hsh[r] = hbuf[r:r + span, :]
    off = CONV_HALO - CONV_K // 2
    for rc in range(TM // CONV_ROWS):
        acc = jnp.zeros((CONV_ROWS, C), F32) + b_ref[...]
        for j in range(CONV_K):
            a, r = divmod(off + j, 8)
            acc = acc + hsh[r, rc * CONV_ROWS + 8 * a:rc * CONV_ROWS + 8 * a + CONV_ROWS, :] * w_ref[j:j + 1, :]
        mu = jnp.mean(acc, axis=-1, keepdims=True)
        xc = acc - mu
        var = jnp.mean(xc * xc, axis=-1, keepdims=True)
        y = xc * lax.rsqrt(var + NORM_EPS) * g_ref[...] + be_ref[...]
        o_ref[0, rc * CONV_ROWS:(rc + 1) * CONV_ROWS, :] = (y * _sigmoid(y)).astype(BF16)


def conv_module(cu, conv_w, conv_b, ln_g, ln_b):
    B, Tt, W = cu.shape
    nt = Tt // TM
    hb = TM // CONV_HALO
    nhb = Tt // CONV_HALO
    C = CONV_CH
    wpad = jnp.zeros((32, C), F32).at[:CONV_K].set(conv_w)
    vec = lambda a: a.reshape(1, C)
    const = lambda shape: pl.BlockSpec(shape, lambda b, t: (0,) * len(shape))
    return pl.pallas_call(
        functools.partial(_conv_kernel, nt=nt),
        grid=(B, nt),
        in_specs=[pl.BlockSpec((1, TM, W), lambda b, t: (b, t, 0)),
                  pl.BlockSpec((1, CONV_HALO, W), lambda b, t: (b, jnp.maximum(t * hb - 1, 0), 0)),
                  pl.BlockSpec((1, CONV_HALO, W), lambda b, t: (b, jnp.minimum((t + 1) * hb, nhb - 1), 0)),
                  const((32, C)), const((1, C)), const((1, C)), const((1, C))],
        out_specs=pl.BlockSpec((1, TM, C), lambda b, t: (b, t, 0)),
        out_shape=jax.ShapeDtypeStruct((B, Tt, C), BF16),
        scratch_shapes=[pltpu.VMEM((TM + 2 * CONV_HALO, C), F32), pltpu.VMEM((8, TM + 2 * CONV_HALO - 8, C), F32)],
        compiler_params=_cparams(2), name="conv_module",
    )(cu, cu, cu, wpad, vec(conv_b), vec(ln_g), vec(ln_b))


def _merge_kernel(na_ref, hfw_ref, hbw_ref, mo_ref, cv_ref, bg_ref, x_ref, mod_ref, ng_ref,
                  wna_ref, wm_ref, wc_ref, wo_ref, wr_ref, x1_ref, h2_ref, aff_ref, *, D):
    mod = mod_ref[0, 0]
    mh = hfw_ref[0] + hbw_ref[0]
    parts = []
    for hh in range(M_HEADS):
        xh = mh[:, hh * M_HEAD_DIM:(hh + 1) * M_HEAD_DIM]
        parts.append(xh * lax.rsqrt(jnp.mean(xh * xh, axis=-1, keepdims=True) + NORM_EPS))
    mn = (jnp.concatenate(parts, axis=1) * ng_ref[...] * _sigmoid(mo_ref[0].astype(F32))).astype(BF16)
    z = (_sigmoid(bg_ref[0, :, 0:D].astype(F32)) * jnp.dot(na_ref[0], wna_ref[...], preferred_element_type=F32)
         + _sigmoid(bg_ref[0, :, D:2 * D].astype(F32)) * jnp.dot(mn, wm_ref[...], preferred_element_type=F32)
         + _sigmoid(bg_ref[0, :, 2 * D:3 * D].astype(F32)) * jnp.dot(cv_ref[0], wc_ref[...], preferred_element_type=F32))
    y = jnp.dot(z.astype(BF16), wo_ref[...], preferred_element_type=F32)
    x1 = x_ref[0] + mod[:, 2 * D:3 * D] * y
    x1_ref[0] = x1
    h2 = _modnorm(x1, mod[:, 3 * D:4 * D], mod[:, 4 * D:5 * D]).astype(BF16)
    h2_ref[0] = h2
    logits = lax.dot_general(wr_ref[...], h2, NT_DIMS, preferred_element_type=F32)
    e = jnp.exp(logits - jnp.max(logits, axis=0, keepdims=True))
    aff_ref[0] = e / jnp.sum(e, axis=0, keepdims=True)


def merge_and_route(na, hfw, hbw, mo, cv, bg, X, modsel, norm_gain, w_na, w_m, w_c, w_o, w_rt):
    B, Tt, D = X.shape
    E = w_rt.shape[0]
    nt = Tt // TM
    tok = lambda w: pl.BlockSpec((1, TM, w), lambda b, t: (b, t, 0))
    const = lambda shape: pl.BlockSpec(shape, lambda b, t: (0,) * len(shape))
    return pl.pallas_call(
        functools.partial(_merge_kernel, D=D),
        grid=(B, nt),
        in_specs=[tok(NA_W), tok(M_W), tok(M_W), tok(M_W), tok(CONV_CH), tok(3 * D), tok(D),
                  pl.BlockSpec((1, 1, 1, 6 * D), lambda b, t: (b, jnp.minimum(t, 1), 0, 0)),
                  const((1, M_W)), const((NA_W, D)), const((M_W, D)), const((CONV_CH, D)), const((D, D)),
                  const((E, D))],
        out_specs=[tok(D), tok(D), pl.BlockSpec((1, E, TM), lambda b, t: (b, 0, t))],
        out_shape=[jax.ShapeDtypeStruct((B, Tt, D), F32), jax.ShapeDtypeStruct((B, Tt, D), BF16),
                   jax.ShapeDtypeStruct((B, E, Tt), F32)],
        compiler_params=_cparams(2), name="merge_and_route",
    )(na, hfw, hbw, mo, cv, bg, X, modsel, norm_gain, w_na, w_m, w_c, w_o, w_rt)


def _topk_kernel(aff_ref, pos_ref, cnt_ref, *, L, k_ctx, k_lat):
    E, Tt = aff_ref.shape[1:]
    u = pltpu.bitcast(aff_ref[0], jnp.int32)
    is_ctx = lax.broadcasted_iota(jnp.int32, (E, Tt), 1) < L

    def count(mask):
        f = jnp.where(mask, 1.0, 0.0)
        c_ctx = jnp.sum(jnp.where(is_ctx, f, 0.0), axis=1, keepdims=True)
        return c_ctx, jnp.sum(f, axis=1, keepdims=True) - c_ctx

    def bit_step(i, carry):
        v_ctx, v_lat = carry
        bit = jnp.left_shift(jnp.int32(1), 30 - i)
        c_ctx, c_lat = count(u >= jnp.where(is_ctx, v_ctx | bit, v_lat | bit))
        return (jnp.where(c_ctx >= k_ctx, v_ctx | bit, v_ctx), jnp.where(c_lat >= k_lat, v_lat | bit, v_lat))

    zero = jnp.zeros((E, 1), jnp.int32)
    v_ctx, v_lat = lax.fori_loop(0, 31, bit_step, (zero, zero))
    thr = jnp.where(is_ctx, v_ctx, v_lat)
    g_ctx, g_lat = count(u > thr)
    need_ctx, need_lat = k_ctx - g_ctx, k_lat - g_lat

    r = lax.broadcasted_iota(jnp.int32, (TM, TM), 0)
    c = lax.broadcasted_iota(jnp.int32, (TM, TM), 1)
    tri = jnp.where(r < c, 1.0, 0.0).astype(BF16)
    lane_t = lax.broadcasted_iota(jnp.int32, (E, LANES), 1)
    cnt = jnp.zeros((E, LANES), F32)
    ties = jnp.zeros((E, 1), F32)
    slots = jnp.zeros((E, 1), F32)
    nt = Tt // TM
    for t in range(nt):
        if t == 1:
            ties = jnp.zeros((E, 1), F32)
        need = need_ctx if t == 0 else need_lat
        ut = pltpu.bitcast(aff_ref[0, :, t * TM:(t + 1) * TM], jnp.int32)
        tt = v_ctx if t == 0 else v_lat
        eq = jnp.where(ut == tt, 1.0, 0.0)
        rank = ties + jnp.dot(eq.astype(BF16), tri, preferred_element_type=F32)
        sel = jnp.where(ut > tt, 1.0, jnp.where(rank < need, eq, 0.0))
        pos = slots + jnp.dot(sel.astype(BF16), tri, preferred_element_type=F32)
        pos_ref[0, :, t * TM:(t + 1) * TM] = jnp.where(sel > 0.0, pos, -1.0).astype(jnp.int32)
        cnt = jnp.where(lane_t == t, slots, cnt)
        ties = ties + jnp.sum(eq, axis=1, keepdims=True)
        slots = slots + jnp.sum(sel, axis=1, keepdims=True)
    cnt = jnp.where(lane_t == nt, slots, cnt)
    cnt_ref[0] = cnt.astype(jnp.int32)


def expert_choice_slots(aff):
    B, E, Tt = aff.shape
    L = CTX_LEN
    k_ctx = EC_FACTOR * L // N_EXPERTS
    k_lat = EC_FACTOR * (Tt - L) // N_EXPERTS
    assert Tt // TM + 1 <= LANES
    return pl.pallas_call(
        functools.partial(_topk_kernel, L=L, k_ctx=k_ctx, k_lat=k_lat),
        grid=(B,),
        in_specs=[pl.BlockSpec((1, E, Tt), lambda b: (b, 0, 0))],
        out_specs=[pl.BlockSpec((1, E, Tt), lambda b: (b, 0, 0)), pl.BlockSpec((1, E, LANES), lambda b: (b, 0, 0))],
        out_shape=[jax.ShapeDtypeStruct((B, E, Tt), jnp.int32), jax.ShapeDtypeStruct((B, E, LANES), jnp.int32)],
        compiler_params=_cparams(1), name="expert_choice_slots",
    )(aff)


def _ffn_kernel(cnt_ref, h_ref, pos_ref, wg_ref, wu_ref, wd_ref, y_ref, xin_ref, acc_ref, *, ns, nt, unroll):
    b, e, f = pl.program_id(0), pl.program_id(1), pl.program_id(2)
    W = SLOT_WIN

    @pl.when(f == 0)
    def _gather():
        xin_ref[...] = jnp.zeros_like(xin_ref)
        acc_ref[...] = jnp.zeros_like(acc_ref)

        def window_rows(t, base):
            tok0 = pl.multiple_of(t * TM, TM)
            srow = base + lax.broadcasted_iota(jnp.int32, (W, TM), 0)
            onehot = jnp.where(pos_ref[0, 0, :, pl.ds(tok0, TM)] == srow, 1.0, 0.0).astype(BF16)
            return jnp.dot(onehot, h_ref[0, pl.ds(tok0, TM), :], preferred_element_type=F32)

        def add_rows(base, rows):
            xin_ref[pl.ds(base, W), :] = (xin_ref[pl.ds(base, W), :].astype(F32) + rows).astype(BF16)

        def group_body(g, carry):
            tiles = [g * unroll + k for k in range(unroll)]
            bases = [pl.multiple_of((cnt_ref[b, e, t] // 16) * 16, 16) for t in tiles]
            first = [window_rows(t, base) for t, base in zip(tiles, bases)]
            for t, base, rows in zip(tiles, bases, first):
                add_rows(base, rows)

                def more(r, c2, t=t, base=base):
                    nxt = pl.multiple_of(base + r * W, 16)
                    add_rows(nxt, window_rows(t, nxt))
                    return c2

                lax.fori_loop(1, (cnt_ref[b, e, t + 1] - base + W - 1) // W, more, 0)
            return carry

        lax.fori_loop(0, nt // unroll, group_body, 0)

    x = xin_ref[0:ns, :]
    a = jnp.dot(x, wg_ref[0, 0].astype(BF16), preferred_element_type=F32)
    u = jnp.dot(x, wu_ref[0, 0].astype(BF16), preferred_element_type=F32)
    hmid = (a * _sigmoid(a) * u).astype(BF16)
    acc_ref[...] += jnp.dot(hmid, wd_ref[0, 0].astype(BF16), preferred_element_type=F32)

    @pl.when(f == pl.num_programs(2) - 1)
    def _():
        y_ref[0, 0, 0:ns, :] = acc_ref[...].astype(BF16)
        y_ref[0, 0, ns:ns + SLOT_WIN, :] = jnp.zeros((SLOT_WIN, y_ref.shape[-1]), BF16)


def expert_ffn(cnt, h2, pos, w_gate, w_up, w_down, layer):
    B, Tt, D = h2.shape
    _, E, _, F = w_gate.shape
    L = CTX_LEN
    ns = EC_FACTOR * L // N_EXPERTS + EC_FACTOR * (Tt - L) // N_EXPERTS
    assert ns % 16 == 0
    tf = min(512, F)
    nt = Tt // TM
    unroll = max(u for u in range(1, 12) if nt % u == 0)
    grid_spec = pltpu.PrefetchScalarGridSpec(
        num_scalar_prefetch=1,
        grid=(B, E, F // tf),
        in_specs=[pl.BlockSpec((1, Tt, D), lambda b, e, f, c: (b, 0, 0), pipeline_mode=pl.Buffered(1)),
                  pl.BlockSpec((1, 1, 1, Tt), lambda b, e, f, c: (b, e, 0, 0)),
                  pl.BlockSpec((1, 1, D, tf), lambda b, e, f, c: (layer, e, 0, f)),
                  pl.BlockSpec((1, 1, D, tf), lambda b, e, f, c: (layer, e, 0, f)),
                  pl.BlockSpec((1, 1, tf, D), lambda b, e, f, c: (layer, e, f, 0))],
        out_specs=pl.BlockSpec((1, 1, ns + SLOT_WIN, D), lambda b, e, f, c: (b, e, 0, 0)),
        scratch_shapes=[pltpu.VMEM((ns + SLOT_WIN, D), BF16), pltpu.VMEM((ns, D), F32)])
    return pl.pallas_call(
        functools.partial(_ffn_kernel, ns=ns, nt=nt, unroll=unroll),
        grid_spec=grid_spec,
        out_shape=jax.ShapeDtypeStruct((B, E, ns + SLOT_WIN, D), BF16),
        compiler_params=_cparams(3), name="expert_ffn",
    )(cnt, h2, pos.reshape(B, E, 1, Tt), w_gate, w_up, w_down)


def _combine_kernel(cnt_ref, pos_ref, aff_ref, y_ref, x_ref, mod_ref, o_ref, yw_ref, p_ref, acc_ref, *, D, ns, t0):
    b, t = pl.program_id(0), pl.program_id(1) + t0
    E = pos_ref.shape[1]
    W = SLOT_WIN
    starts, rounds = [], jnp.int32(0)
    for e in range(E):
        lo, hi = cnt_ref[b, e, t], cnt_ref[b, e, t + 1]
        s0 = (lo // 16) * 16
        starts.append(s0)
        rounds = jnp.maximum(rounds, jnp.where(hi > lo, (hi - s0 + W - 1) // W, 0))
    acc_ref[...] = jnp.zeros_like(acc_ref)

    def round_body(r, carry):
        for e in range(E):
            base = pl.multiple_of(jnp.minimum(starts[e] + r * W, ns), 16)
            yw_ref[e * W:(e + 1) * W, :] = y_ref[0, e, pl.ds(base, W), :]
            srow = base + lax.broadcasted_iota(jnp.int32, (W, TM), 0)
            p_ref[e * W:(e + 1) * W, :] = jnp.where(pos_ref[0, e:e + 1, :] == srow, aff_ref[0, e:e + 1, :], 0.0).astype(BF16)
        acc_ref[...] += lax.dot_general(p_ref[...], yw_ref[...], TN_DIMS, preferred_element_type=F32)
        return carry

    lax.fori_loop(0, rounds, round_body, 0)
    o_ref[0] = x_ref[0] + mod_ref[0, 0][:, 5 * D:6 * D] * acc_ref[...]


def combine(cnt, pos, aff, Y, X1, modsel, latent_only):
    B, Tt, D = X1.shape
    E = pos.shape[1]
    nsp = Y.shape[2]
    t0 = 1 if latent_only else 0
    grid_spec = pltpu.PrefetchScalarGridSpec(
        num_scalar_prefetch=1,
        grid=(B, Tt // TM - t0),
        in_specs=[pl.BlockSpec((1, E, TM), lambda b, t, c: (b, 0, t + t0)),
                  pl.BlockSpec((1, E, TM), lambda b, t, c: (b, 0, t + t0)),
                  pl.BlockSpec((1, E, nsp, D), lambda b, t, c: (b, 0, 0, 0), pipeline_mode=pl.Buffered(1)),
                  pl.BlockSpec((1, TM, D), lambda b, t, c: (b, t + t0, 0)),
                  pl.BlockSpec((1, 1, 1, 6 * D), lambda b, t, c: (b, jnp.minimum(t + t0, 1), 0, 0))],
        out_specs=pl.BlockSpec((1, TM, D), lambda b, t, c: (b, t, 0)),
        scratch_shapes=[pltpu.VMEM((E * SLOT_WIN, D), BF16), pltpu.VMEM((E * SLOT_WIN, TM), BF16),
                        pltpu.VMEM((TM, D), F32)])
    return pl.pallas_call(
        functools.partial(_combine_kernel, D=D, ns=nsp - SLOT_WIN, t0=t0),
        grid_spec=grid_spec,
        out_shape=jax.ShapeDtypeStruct((B, Tt - t0 * TM, D), F32),
        compiler_params=_cparams(2), name="combine",
    )(cnt, pos, aff, Y, X1, modsel)


def _pack_w_in(w, D):
    widths = (NA_W, NA_W, NA_W, M_W, M_W, M_W, M_W, N_GATE_COLS, 2 * CONV_CH, 3 * D)
    offs = [0]
    for wd in widths:
        offs.append(offs[-1] + wd)
    seg = [w[:, offs[i]:offs[i + 1]] for i in range(len(widths))]
    return jnp.concatenate(seg[:7] + seg[8:], axis=1).astype(BF16), _gate_rows(seg[7].T).astype(BF16)


def _gate_rows(g):
    H = M_HEADS
    i_fw, f_fw, i_bw, f_bw = (g[j * H:(j + 1) * H] for j in range(4))
    return jnp.concatenate([i_fw, f_fw, f_fw, i_fw, i_bw, f_bw, f_bw, i_bw], axis=0)


def _rope_tables(T, L):
    nf = M_HEAD_DIM // 4
    t = jnp.arange(T)
    row = (t // GRID_W).astype(F32)
    col = (t % GRID_W).astype(F32)
    inv = ROPE_BASE ** (-jnp.arange(nf, dtype=F32) / nf)
    ar, ac = row[:, None] * inv[None, :], col[:, None] * inv[None, :]
    z = jnp.zeros((T, nf), F32)
    cos = jnp.concatenate([jnp.cos(ar), jnp.cos(ar), jnp.cos(ac), jnp.cos(ac)], axis=1)
    sina = jnp.concatenate([-jnp.sin(ar), z, -jnp.sin(ac), z], axis=1)
    sinb = jnp.concatenate([z, jnp.sin(ar), z, jnp.sin(ac)], axis=1)
    ident = lambda v: jnp.full((L, M_HEAD_DIM), v, F32)
    return (jnp.concatenate([ident(1.0), cos]), jnp.concatenate([ident(0.0), sina]),
            jnp.concatenate([ident(0.0), sinb]))


def kernel(x, c, ctx, c_ctx, w_mod, b_mod, w_in, na_q_gain, na_k_gain, na_rpb, m_gate_bias,
           m_norm_gain, conv_w, conv_b, conv_ln_g, conv_ln_b, w_br_na, w_br_m, w_br_conv, w_out,
           w_router, w_gate, w_up, w_down):
    B, T, D = x.shape
    L = ctx.shape[1]
    assert L == CTX_LEN == TM and T % TM == 0 and B + 1 <= 16
    depth = w_in.shape[0]
    X = jnp.concatenate([ctx, x], axis=1)
    cc = jnp.zeros((16, D), F32).at[:B].set(c).at[B].set(c_ctx)
    mods = modulation(cc, w_mod, b_mod)
    cos, sina, sinb = _rope_tables(T, L)
    hd = jnp.arange(NA_W) // NA_HEAD_DIM
    gmat = (hd[:, None] == hd[None, :]).astype(BF16)
    bias_tables = na_bias_tables(na_rpb)
    for l in range(depth):
        modsel = jnp.stack([jnp.broadcast_to(mods[l, B], (B, 6 * D)), mods[l, :B]], axis=1).reshape(B, 2, 1, 6 * D)
        q_gain = (jnp.tile(na_q_gain[l], NA_HEADS) * NA_HEAD_DIM ** -0.5).reshape(1, NA_W)
        k_gain = jnp.tile(na_k_gain[l], NA_HEADS).reshape(1, NA_W)
        w_tok, w_gates_t = _pack_w_in(w_in[l], D)
        naq, nak, nav, mq, mk, mv, mo, mgf, mgb, cu, bg = input_projection(
            X, modsel, w_tok, q_gain, k_gain, gmat, cos, sina, sinb, w_gates_t,
            _gate_rows(m_gate_bias[l].reshape(N_GATE_COLS, 1)))
        na = na_attention(naq, nak, nav, bias_tables[l])
        hfw, hbw = mlstm(mq, mk, mv, mgf, mgb)
        cv = conv_module(cu, conv_w[l], conv_b[l], conv_ln_g[l], conv_ln_b[l])
        X1, h2, aff = merge_and_route(
            na, hfw, hbw, mo, cv, bg, X, modsel, m_norm_gain[l].reshape(1, M_W),
            w_br_na[l].astype(BF16), w_br_m[l].astype(BF16), w_br_conv[l].astype(BF16), w_out[l].astype(BF16),
            w_router[l].T.astype(BF16))
        pos, cnt = expert_choice_slots(aff)
        cnt = cnt[:, :, :T // TM + 2]
        Y = expert_ffn(cnt, h2, pos, w_gate, w_up, w_down, l)
        X = combine(cnt, pos, aff, Y, X1, modsel, latent_only=(l == depth - 1))
    return X
```

```python
import functools

import jax
import jax.numpy as jnp
from jax import lax
from jax.experimental import pallas as pl
from jax.experimental.pallas import tpu as pltpu

DEPTH = 4
CTX_LEN = 256
GRID_W = 64
NA_HEADS = 8
NA_HEAD_DIM = 64
NA_KH = 8
NA_KW = 16
M_HEADS = 4
M_HEAD_DIM = 128
CONV_CH = 512
CONV_K = 31
N_EXPERTS = 16
EC_FACTOR = 2
ROPE_BASE = 10000.0
NORM_EPS = 1e-6

NA_W = NA_HEADS * NA_HEAD_DIM
M_W = M_HEADS * M_HEAD_DIM
N_GATE_COLS = 4 * M_HEADS

LANES = 128
TM = 256
M_CH = 128
CONV_HALO = 16
CONV_ROWS = 64
SLOT_WIN = 64
NEG = -1e30
LOG2E = 1.4426950408889634
VMEM_LIMIT_BYTES = 56 * 1024 * 1024

F32 = jnp.float32
BF16 = jnp.bfloat16
NT_DIMS = (((1,), (1,)), ((), ()))
TN_DIMS = (((0,), (0,)), ((), ()))


def _cparams(n_axes):
    return pltpu.CompilerParams(dimension_semantics=("arbitrary",) * n_axes,
                                vmem_limit_bytes=VMEM_LIMIT_BYTES)


def _sigmoid(x):
    return 1.0 / (1.0 + jnp.exp(-x))


def _modnorm(x, shift, scale):
    ms = jnp.mean(x * x, axis=-1, keepdims=True)
    return (x * lax.rsqrt(ms + NORM_EPS)) * (1.0 + scale) + shift


def _chunk(width):
    return 512 if width % 512 == 0 else 256


def _mod_kernel(c_ref, w_ref, b_ref, o_ref):
    c = c_ref[...]
    s = (c * _sigmoid(c)).astype(BF16)
    o_ref[0] = jnp.dot(s, w_ref[0].astype(BF16), preferred_element_type=F32) + b_ref[0]


def modulation(cc, w_mod, b_mod):
    depth, D, N = w_mod.shape
    tn = 1024 if N % 1024 == 0 else N
    return pl.pallas_call(
        _mod_kernel,
        grid=(depth, N // tn),
        in_specs=[pl.BlockSpec((16, D), lambda l, n: (0, 0)),
                  pl.BlockSpec((1, D, tn), lambda l, n: (l, 0, n)),
                  pl.BlockSpec((1, 1, tn), lambda l, n: (l, 0, n))],
        out_specs=pl.BlockSpec((1, 16, tn), lambda l, n: (l, 0, n)),
        out_shape=jax.ShapeDtypeStruct((depth, 16, N), F32),
        compiler_params=_cparams(2), name="modulation",
    )(cc, w_mod, b_mod.reshape(depth, 1, N))


def _inproj_kernel(x_ref, mod_ref, w_ref, qg_ref, kg_ref, gm_ref, cos_ref, sa_ref, sb_ref, wgt_ref, gb_ref,
                   naq_ref, nak_ref, nav_ref, mq_ref, mk_ref, mv_ref, mo_ref, mgf_ref, mgb_ref, cu_ref, bg_ref, *, D):
    mod = mod_ref[0, 0]
    h = _modnorm(x_ref[0], mod[:, 0:D], mod[:, D:2 * D]).astype(BF16)

    def proj(off, width):
        return jnp.dot(h, w_ref[:, off:off + width], preferred_element_type=F32)

    gm = gm_ref[...]
    for ref, gain_ref, off in ((naq_ref, qg_ref, 0), (nak_ref, kg_ref, NA_W)):
        y = proj(off, NA_W)
        ysq = y * y
        hi = ysq.astype(BF16)
        lo = (ysq - hi.astype(F32)).astype(BF16)
        ss = jnp.dot(hi, gm, preferred_element_type=F32) + jnp.dot(lo, gm, preferred_element_type=F32)
        ref[0] = (y * lax.rsqrt(ss * (1.0 / NA_HEAD_DIM) + NORM_EPS) * gain_ref[...]).astype(BF16)
    nav_ref[0] = proj(2 * NA_W, NA_W).astype(BF16)

    cos, sa, sb = cos_ref[...], sa_ref[...], sb_ref[...]
    q4 = M_HEAD_DIM // 4
    base = 3 * NA_W
    for ref, off, scale in ((mq_ref, base, 1.0), (mk_ref, base + M_W, M_HEAD_DIM ** -0.5)):
        y = proj(off, M_W)
        for hh in range(M_HEADS):
            yh = y[:, hh * M_HEAD_DIM:(hh + 1) * M_HEAD_DIM]
            r = yh * cos + pltpu.roll(yh, M_HEAD_DIM - q4, 1) * sa + pltpu.roll(yh, q4, 1) * sb
            ref[0, :, hh * M_HEAD_DIM:(hh + 1) * M_HEAD_DIM] = (r * scale).astype(BF16)
    mv_ref[0] = proj(base + 2 * M_W, M_W).astype(BF16)
    mo_ref[0] = proj(base + 3 * M_W, M_W).astype(BF16)
    base += 4 * M_W
    for ref, width in ((cu_ref, 2 * CONV_CH), (bg_ref, 3 * D)):
        cw = _chunk(width)
        for j in range(width // cw):
            ref[0, :, j * cw:(j + 1) * cw] = proj(base + j * cw, cw).astype(BF16)
        base += width
    g = lax.dot_general(wgt_ref[...], h, NT_DIMS, preferred_element_type=F32) + gb_ref[...]
    nrow = g.shape[0] // 2
    for cidx in range(g.shape[1] // M_CH):
        mgf_ref[0, cidx] = g[0:nrow, cidx * M_CH:(cidx + 1) * M_CH]
        mgb_ref[0, cidx] = g[nrow:2 * nrow, cidx * M_CH:(cidx + 1) * M_CH]


def input_projection(X, modsel, w_packed, q_gain, k_gain, gmat, cos, sina, sinb, w_gates_t, gate_bias):
    B, Tt, D = X.shape
    NP = w_packed.shape[1]
    nt = Tt // TM
    tok = lambda w: pl.BlockSpec((1, TM, w), lambda b, t: (b, t, 0))
    const = lambda shape: pl.BlockSpec(shape, lambda b, t: (0,) * len(shape))
    out_widths = (NA_W, NA_W, NA_W, M_W, M_W, M_W, M_W, None, None, 2 * CONV_CH, 3 * D)
    gate_rows = w_gates_t.shape[0] // 2
    gate_spec = pl.BlockSpec((1, TM // M_CH, gate_rows, M_CH), lambda b, t: (b, t, 0, 0))
    return pl.pallas_call(
        functools.partial(_inproj_kernel, D=D),
        grid=(B, nt),
        in_specs=[tok(D),
                  pl.BlockSpec((1, 1, 1, 6 * D), lambda b, t: (b, jnp.minimum(t, 1), 0, 0)),
                  pl.BlockSpec((D, NP), lambda b, t: (0, 0), pipeline_mode=pl.Buffered(1)),
                  const((1, NA_W)), const((1, NA_W)), const((NA_W, NA_W)),
                  pl.BlockSpec((TM, M_HEAD_DIM), lambda b, t: (t, 0)),
                  pl.BlockSpec((TM, M_HEAD_DIM), lambda b, t: (t, 0)),
                  pl.BlockSpec((TM, M_HEAD_DIM), lambda b, t: (t, 0)),
                  const((2 * gate_rows, D)), const((2 * gate_rows, 1))],
        out_specs=[gate_spec if w is None else tok(w) for w in out_widths],
        out_shape=[jax.ShapeDtypeStruct((B, Tt // M_CH, gate_rows, M_CH), F32) if w is None
                   else jax.ShapeDtypeStruct((B, Tt, w), BF16) for w in out_widths],
        compiler_params=_cparams(2), name="input_projection",
    )(X, modsel, w_packed, q_gain, k_gain, gmat, cos, sina, sinb, w_gates_t, gate_bias)


def _na_kernel(q_ref, k_ref, v_ref, bias_ref, o_ref, *, L, rows):
    t = pl.program_id(2)
    even = lax.broadcasted_iota(jnp.int32, (1, LANES), 1) < NA_HEAD_DIM
    pairs = [slice(jj * LANES, (jj + 1) * LANES) for jj in range(NA_PAIRS)]

    def heads_stack(q):
        z = jnp.zeros_like(q)
        return jnp.concatenate([jnp.where(even, q, z), jnp.where(even, z, q)], axis=0)

    def heads_merge(o):
        n = o.shape[0] // 2
        return jnp.where(even, o[:n], o[n:])

    @pl.when(t == 0)
    def _context_queries():
        for sl in pairs:
            kc, vc = k_ref[0, 0:L, sl], v_ref[0, 0:L, sl]
            q2 = heads_stack(q_ref[0, :, sl])
            s = lax.dot_general(q2, kc, NT_DIMS, preferred_element_type=F32)
            m = jnp.max(s, axis=-1, keepdims=True)
            p = jnp.exp2(s - m)
            l = jnp.sum(p, axis=-1, keepdims=True)
            o = jnp.dot(p.astype(BF16), vc, preferred_element_type=F32) / l
            o_ref[0, :, sl] = heads_merge(o).astype(BF16)

    @pl.when(t > 0)
    def _latent_queries():
        for rr in range(TM // GRID_W):
            r = (t - 1) * (TM // GRID_W) + rr
            rs = jnp.clip(r - NA_KH // 2, 0, rows - NA_KH)
            start = pl.multiple_of(L + rs * GRID_W, GRID_W)
            qrows = slice(rr * GRID_W, (rr + 1) * GRID_W)
            for jj, sl in enumerate(pairs):
                kc, vc = k_ref[0, 0:L, sl], v_ref[0, 0:L, sl]
                kw = k_ref[0, pl.ds(start, NA_KH * GRID_W), sl]
                vw = v_ref[0, pl.ds(start, NA_KH * GRID_W), sl]
                q2 = heads_stack(q_ref[0, qrows, sl])
                s_lat = lax.dot_general(q2, kw, NT_DIMS, preferred_element_type=F32) + bias_ref[jj, r - rs]
                s_ctx = lax.dot_general(q2, kc, NT_DIMS, preferred_element_type=F32)
                m = jnp.maximum(jnp.max(s_lat, axis=-1, keepdims=True), jnp.max(s_ctx, axis=-1, keepdims=True))
                p_lat = jnp.exp2(s_lat - m)
                p_ctx = jnp.exp2(s_ctx - m)
                l = jnp.sum(p_lat, axis=-1, keepdims=True) + jnp.sum(p_ctx, axis=-1, keepdims=True)
                o = (jnp.dot(p_lat.astype(BF16), vw, preferred_element_type=F32)
                     + jnp.dot(p_ctx.astype(BF16), vc, preferred_element_type=F32)) / l
                o_ref[0, qrows, sl] = heads_merge(o).astype(BF16)


NA_PAIRS = 4


def na_attention(q, k, v, bias):
    B, Tt, _ = q.shape
    L = CTX_LEN
    rows = (Tt - L) // GRID_W
    w = NA_PAIRS * LANES
    return pl.pallas_call(
        functools.partial(_na_kernel, L=L, rows=rows),
        grid=(B, NA_W // w, Tt // TM),
        in_specs=[pl.BlockSpec((1, TM, w), lambda b, j, t: (b, t, j)),
                  pl.BlockSpec((1, Tt, w), lambda b, j, t: (b, 0, j), pipeline_mode=pl.Buffered(1)),
                  pl.BlockSpec((1, Tt, w), lambda b, j, t: (b, 0, j), pipeline_mode=pl.Buffered(1)),
                  pl.BlockSpec((NA_PAIRS,) + bias.shape[1:], lambda b, j, t: (j, 0, 0, 0),
                               pipeline_mode=pl.Buffered(1))],
        out_specs=pl.BlockSpec((1, TM, w), lambda b, j, t: (b, t, j)),
        out_shape=jax.ShapeDtypeStruct((B, Tt, NA_W), BF16),
        compiler_params=_cparams(3), name="na_attention",
    )(q, k, v, bias)


def na_bias_tables(rpb):
    depth, H = rpb.shape[:2]
    cq = jnp.arange(GRID_W)
    col_start = jnp.clip(cq - NA_KW // 2, 0, GRID_W - NA_KW)
    col_mask = (cq[None, :] >= col_start[:, None]) & (cq[None, :] < col_start[:, None] + NA_KW)
    col_idx = jnp.clip(cq[None, :] - cq[:, None] + NA_KW - 1, 0, 2 * NA_KW - 2)
    pick = (col_idx[None] == jnp.arange(2 * NA_KW - 1)[:, None, None]).astype(F32)
    cols = jnp.einsum('lhrc,cqk->lhrqk', rpb, pick, precision=lax.Precision.HIGHEST)
    cols = jnp.where(col_mask, cols, NEG)
    tbl = jnp.stack([cols[:, :, NA_KH - 1 - dl:2 * NA_KH - 1 - dl] for dl in range(NA_KH)], axis=2)
    tbl = jnp.transpose(tbl, (0, 1, 2, 4, 3, 5)).reshape(depth, H // 2, 2, NA_KH, GRID_W, NA_KH * GRID_W)
    return jnp.transpose(tbl, (0, 1, 3, 2, 4, 5)).reshape(depth, H // 2, NA_KH, 2 * GRID_W, NA_KH * GRID_W)


def _log_sigmoid(x):
    return jnp.minimum(x, 0.0) - jnp.log(1.0 + jnp.exp(-jnp.abs(x)))


def _mlstm_gate_kernel(gf_ref, gb_ref, rqf_ref, rqb_ref, col_ref, t_scr):
    nc, n = gf_ref.shape[1], M_CH
    rows = nc * 8
    lane = lax.broadcasted_iota(jnp.int32, (rows, n), 1)
    for d, (g_ref, rq_ref) in enumerate(((gf_ref, rqf_ref), (gb_ref, rqb_ref))):
        def scan(x, op, fill):
            sh = 1
            while sh < n:
                if d == 0:
                    x = op(x, jnp.where(lane >= sh, pltpu.roll(x, sh, 1), fill))
                else:
                    x = op(x, jnp.where(lane < n - sh, pltpu.roll(x, n - sh, 1), fill))
                sh *= 2
            return x

        top = g_ref[0, :, 0:8, :].reshape(rows, n)
        bot = g_ref[0, :, 8:16, :].reshape(rows, n)
        b = scan(_log_sigmoid(bot), jnp.add, 0.0)
        last = n - 1 if d == 0 else 0
        tot = jnp.broadcast_to(b[:, last:last + 1], (rows, n))
        r = top - b
        cm = scan(r, jnp.maximum, NEG)
        amax = jnp.broadcast_to(jnp.max(tot + r, axis=1, keepdims=True), (rows, n))
        for j, val in enumerate((r, cm, tot, amax)):
            rq_ref[0, :, 8 * j:8 * (j + 1), :] = val.reshape(nc, 8, n)
        t_scr[:, 16 * d:16 * d + 8, :] = b.reshape(nc, 8, n)
        t_scr[:, 16 * d + 8:16 * d + 16, :] = cm.reshape(nc, 8, n)

    def transpose_chunk(ci, carry):
        tile = jnp.concatenate([t_scr[ci], jnp.zeros((n - 32, n), F32)], axis=0)
        col_ref[0, ci] = tile.T
        return carry

    lax.fori_loop(0, nc, transpose_chunk, 0)


def mlstm_gates(gf, gb):
    B, nc, _, n = gf.shape
    blk = lambda r: pl.BlockSpec((1, nc, r, n), lambda b: (b, 0, 0, 0))
    return pl.pallas_call(
        _mlstm_gate_kernel,
        grid=(B,),
        in_specs=[blk(16), blk(16)],
        out_specs=[blk(32), blk(32), blk(n)],
        out_shape=[jax.ShapeDtypeStruct((B, nc, 32, n), F32)] * 2 + [jax.ShapeDtypeStruct((B, nc, n, n), F32)],
        scratch_shapes=[pltpu.VMEM((nc, 32, n), F32)],
        compiler_params=_cparams(1), name="mlstm_gates",
    )(gf, gb)


def _mlstm_chain_group(d, q_ref, k_ref, v_ref, rq_ref, col_ref, o_ref, ct_ref, m_ref, bb):
    n, H = M_CH, M_HEADS
    row8 = lax.broadcasted_iota(jnp.int32, (2 * H, n), 0)
    lane = lax.broadcasted_iota(jnp.int32, (n, n), 1)
    sub = lax.broadcasted_iota(jnp.int32, (n, n), 0)
    eye = lane == sub
    causal = (lane <= sub) if d == 0 else (lane >= sub)
    one0 = jnp.where(lane == 0, 1.0, 0.0).astype(BF16)

    r8, cm8 = rq_ref[bb, 0, 0:8, :], rq_ref[bb, 0, 8:16, :]
    tot, amax = rq_ref[bb, 0, 16:24, :], rq_ref[bb, 0, 24:32, :]
    m0 = m_ref[d, bb]
    nm8 = jnp.maximum(m0, cm8)
    m_new = jnp.maximum(tot + m0, amax)
    decay8 = jnp.exp(tot + m0 - m_new)
    w8 = jnp.exp(tot + r8 - m_new)
    e8 = jnp.exp(m0 - nm8)
    m_ref[d, bb] = jnp.where(row8 < H, m_new, 0.0)
    tc = col_ref[bb, 0]

    for hh in range(H):
        sl = slice(hh * M_HEAD_DIM, (hh + 1) * M_HEAD_DIM)
        qh, kh, vh = q_ref[bb, :, sl], k_ref[bb, :, sl], v_ref[bb, :, sl]
        b_col = tc[:, 16 * d + hh:16 * d + hh + 1]
        nm_col = jnp.maximum(m0[hh:hh + 1, 0:1], tc[:, 16 * d + 8 + hh:16 * d + 8 + hh + 1])
        dmat = jnp.where(causal, jnp.exp(r8[hh:hh + 1, :] - nm_col), 0.0)
        s = (lax.dot_general(qh, kh, NT_DIMS, preferred_element_type=F32) * dmat).astype(BF16)
        e_diag = jnp.where(eye, e8[hh:hh + 1, :], 0.0).astype(BF16)
        w_diag = jnp.where(eye, w8[hh:hh + 1, :], 0.0).astype(BF16)
        vaug = jnp.concatenate([vh, one0], axis=1)
        ct = ct_ref[d, bb, hh]
        inter = jnp.dot(qh, ct.astype(BF16), preferred_element_type=F32).astype(BF16)
        num = jnp.dot(jnp.concatenate([s, e_diag], axis=1), jnp.concatenate([vaug, inter], axis=0),
                      preferred_element_type=F32)
        den = num[:, M_HEAD_DIM:M_HEAD_DIM + 1]
        o_ref[bb, :, sl] = num[:, :M_HEAD_DIM] / jnp.maximum(jnp.abs(den), jnp.exp(-(b_col + nm_col)))
        wv = jnp.dot(w_diag, vaug, preferred_element_type=F32).astype(BF16)
        ct_ref[d, bb, hh] = decay8[hh:hh + 1, 0:1] * ct + lax.dot_general(kh, wv, TN_DIMS, preferred_element_type=F32)


def _mlstm_kernel(qf_ref, kf_ref, vf_ref, rqf_ref, colf_ref, qb_ref, kb_ref, vb_ref, rqb_ref, colb_ref,
                  of_ref, ob_ref, ct_ref, m_ref):
    @pl.when(pl.program_id(1) == 0)
    def _():
        ct_ref[...] = jnp.zeros_like(ct_ref)
        m_ref[...] = jnp.zeros_like(m_ref)

    for bb in range(qf_ref.shape[0]):
        _mlstm_chain_group(0, qf_ref, kf_ref, vf_ref, rqf_ref, colf_ref, of_ref, ct_ref, m_ref, bb)
        _mlstm_chain_group(1, qb_ref, kb_ref, vb_ref, rqb_ref, colb_ref, ob_ref, ct_ref, m_ref, bb)


M_BATCH = 4


def mlstm(q, k, v, gf, gb):
    B, Tt, _ = q.shape
    nc = Tt // M_CH
    lc = CTX_LEN // M_CH
    nb = M_BATCH if B % M_BATCH == 0 else 1
    rqf, rqb, col = mlstm_gates(gf, gb)

    def bw(s):
        return jnp.where(s < lc, lc - 1 - s, nc - 1 + lc - s)

    tok_f = pl.BlockSpec((nb, M_CH, M_W), lambda b, s: (b, s, 0))
    tok_b = pl.BlockSpec((nb, M_CH, M_W), lambda b, s: (b, bw(s), 0))
    chunk_f = lambda r: pl.BlockSpec((nb, 1, r, M_CH), lambda b, s: (b, s, 0, 0))
    chunk_b = lambda r: pl.BlockSpec((nb, 1, r, M_CH), lambda b, s: (b, bw(s), 0, 0))
    return pl.pallas_call(
        _mlstm_kernel,
        grid=(B // nb, nc),
        in_specs=[tok_f, tok_f, tok_f, chunk_f(32), chunk_f(M_CH), tok_b, tok_b, tok_b, chunk_b(32), chunk_b(M_CH)],
        out_specs=[tok_f, tok_b],
        out_shape=[jax.ShapeDtypeStruct((B, Tt, M_W), F32)] * 2,
        scratch_shapes=[pltpu.VMEM((2, nb, M_HEADS, M_HEAD_DIM, M_HEAD_DIM + LANES), F32),
                        pltpu.VMEM((2, nb, 2 * M_HEADS, M_CH), F32)],
        compiler_params=_cparams(2), name="mlstm",
    )(q, k, v, rqf, col, q, k, v, rqb, col)


def _conv_tile(t, nt, cur_ref, prev_ref, next_ref, w_ref, b_ref, g_ref, be_ref, o_ref, hbuf, hsh):
    C = CONV_CH

    def glu(u):
        u = u.astype(F32)
        return u[:, :C] * _sigmoid(u[:, C:])

    prev_ok = t >= 2
    next_ok = jnp.logical_and(t >= 1, t < nt - 1)
    hbuf[0:CONV_HALO] = jnp.where(prev_ok, glu(prev_ref[0]), 0.0)
    hbuf[CONV_HALO:CONV_HALO + TM] = glu(cur_ref[0])
    hbuf[CONV_HALO + TM:2 * CONV_HALO + TM] = jnp.where(next_ok, glu(next_ref[0]), 0.0)
    span = TM + 2 * CONV_HALO - 8
    for r in range(8):
        hsh[r] = hbuf[r:r + span, :]
    off = CONV_HALO - CONV_K // 2
    for rc in range(TM // CONV_ROWS):
        acc = jnp.zeros((CONV_ROWS, C), F32) + b_ref[...]
        for j in range(CONV_K):
            a, r = divmod(off + j, 8)
            acc = acc + hsh[r, rc * CONV_ROWS + 8 * a:rc * CONV_ROWS + 8 * a + CONV_ROWS, :] * w_ref[j:j + 1, :]
        mu = jnp.mean(acc, axis=-1, keepdims=True)
        xc = acc - mu
        var = jnp.mean(xc * xc, axis=-1, keepdims=True)
        y = xc * lax.rsqrt(var + NORM_EPS) * g_ref[...] + be_ref[...]
        o_ref[rc * CONV_ROWS:(rc + 1) * CONV_ROWS, :] = (y * _sigmoid(y)).astype(BF16)


def _merge_kernel(na_ref, hfw_ref, hbw_ref, mo_ref, cu_ref, cup_ref, cun_ref, bg_ref, x_ref, mod_ref, ng_ref,
                  cw_ref, cb_ref, cg_ref, cbe_ref, wna_ref, wm_ref, wc_ref, wo_ref, wr_ref,
                  x1_ref, h2_ref, aff_ref, cv_ref, hbuf, hsh, *, D, nt):
    _conv_tile(pl.program_id(1), nt, cu_ref, cup_ref, cun_ref, cw_ref, cb_ref, cg_ref, cbe_ref, cv_ref, hbuf, hsh)
    mod = mod_ref[0, 0]
    mh = hfw_ref[0] + hbw_ref[0]
    parts = []
    for hh in range(M_HEADS):
        xh = mh[:, hh * M_HEAD_DIM:(hh + 1) * M_HEAD_DIM]
        parts.append(xh * lax.rsqrt(jnp.mean(xh * xh, axis=-1, keepdims=True) + NORM_EPS))
    mn = (jnp.concatenate(parts, axis=1) * ng_ref[...] * _sigmoid(mo_ref[0].astype(F32))).astype(BF16)
    z = (_sigmoid(bg_ref[0, :, 0:D].astype(F32)) * jnp.dot(na_ref[0], wna_ref[...], preferred_element_type=F32)
         + _sigmoid(bg_ref[0, :, D:2 * D].astype(F32)) * jnp.dot(mn, wm_ref[...], preferred_element_type=F32)
         + _sigmoid(bg_ref[0, :, 2 * D:3 * D].astype(F32)) * jnp.dot(cv_ref[...], wc_ref[...], preferred_element_type=F32))
    y = jnp.dot(z.astype(BF16), wo_ref[...], preferred_element_type=F32)
    x1 = x_ref[0] + mod[:, 2 * D:3 * D] * y
    x1_ref[0] = x1
    h2 = _modnorm(x1, mod[:, 3 * D:4 * D], mod[:, 4 * D:5 * D]).astype(BF16)
    h2_ref[0] = h2
    logits = lax.dot_general(wr_ref[...], h2, NT_DIMS, preferred_element_type=F32)
    e = jnp.exp(logits - jnp.max(logits, axis=0, keepdims=True))
    aff_ref[0] = e / jnp.sum(e, axis=0, keepdims=True)


def merge_and_route(na, hfw, hbw, mo, cu, bg, X, modsel, norm_gain, conv_w, conv_b, ln_g, ln_b,
                    w_na, w_m, w_c, w_o, w_rt):
    B, Tt, D = X.shape
    E = w_rt.shape[0]
    C = CONV_CH
    nt = Tt // TM
    hb = TM // CONV_HALO
    nhb = Tt // CONV_HALO
    wpad = jnp.zeros((32, C), F32).at[:CONV_K].set(conv_w)
    vec = lambda a: a.reshape(1, C)
    tok = lambda w: pl.BlockSpec((1, TM, w), lambda b, t: (b, t, 0))
    const = lambda shape: pl.BlockSpec(shape, lambda b, t: (0,) * len(shape))
    return pl.pallas_call(
        functools.partial(_merge_kernel, D=D, nt=nt),
        grid=(B, nt),
        in_specs=[tok(NA_W), tok(M_W), tok(M_W), tok(M_W), tok(2 * C),
                  pl.BlockSpec((1, CONV_HALO, 2 * C), lambda b, t: (b, jnp.maximum(t * hb - 1, 0), 0)),
                  pl.BlockSpec((1, CONV_HALO, 2 * C), lambda b, t: (b, jnp.minimum((t + 1) * hb, nhb - 1), 0)),
                  tok(3 * D), tok(D),
                  pl.BlockSpec((1, 1, 1, 6 * D), lambda b, t: (b, jnp.minimum(t, 1), 0, 0)),
                  const((1, M_W)), const((32, C)), const((1, C)), const((1, C)), const((1, C)),
                  const((NA_W, D)), const((M_W, D)), const((C, D)), const((D, D)), const((E, D))],
        out_specs=[tok(D), tok(D), pl.BlockSpec((1, E, TM), lambda b, t: (b, 0, t))],
        out_shape=[jax.ShapeDtypeStruct((B, Tt, D), F32), jax.ShapeDtypeStruct((B, Tt, D), BF16),
                   jax.ShapeDtypeStruct((B, E, Tt), F32)],
        scratch_shapes=[pltpu.VMEM((TM, C), BF16), pltpu.VMEM((TM + 2 * CONV_HALO, C), F32),
                        pltpu.VMEM((8, TM + 2 * CONV_HALO - 8, C), F32)],
        compiler_params=_cparams(2), name="merge_and_route",
    )(na, hfw, hbw, mo, cu, cu, cu, bg, X, modsel, norm_gain, wpad, vec(conv_b), vec(ln_g), vec(ln_b),
      w_na, w_m, w_c, w_o, w_rt)


def _topk_kernel(aff_ref, pos_ref, cnt_ref, *, L, k_ctx, k_lat):
    E, Tt = aff_ref.shape[1:]
    u = pltpu.bitcast(aff_ref[0], jnp.int32)
    is_ctx = lax.broadcasted_iota(jnp.int32, (E, Tt), 1) < L

    def count(mask):
        f = jnp.where(mask, 1.0, 0.0)
        c_ctx = jnp.sum(jnp.where(is_ctx, f, 0.0), axis=1, keepdims=True)
        return c_ctx, jnp.sum(f, axis=1, keepdims=True) - c_ctx

    def bit_step(i, carry):
        v_ctx, v_lat = carry
        bit = jnp.left_shift(jnp.int32(1), 30 - i)
        c_ctx, c_lat = count(u >= jnp.where(is_ctx, v_ctx | bit, v_lat | bit))
        return (jnp.where(c_ctx >= k_ctx, v_ctx | bit, v_ctx), jnp.where(c_lat >= k_lat, v_lat | bit, v_lat))

    zero = jnp.zeros((E, 1), jnp.int32)
    v_ctx, v_lat = lax.fori_loop(0, 31, bit_step, (zero, zero))
    thr = jnp.where(is_ctx, v_ctx, v_lat)
    g_ctx, g_lat = count(u > thr)
    need_ctx, need_lat = k_ctx - g_ctx, k_lat - g_lat

    r = lax.broadcasted_iota(jnp.int32, (TM, TM), 0)
    c = lax.broadcasted_iota(jnp.int32, (TM, TM), 1)
    tri = jnp.where(r < c, 1.0, 0.0).astype(BF16)
    lane_t = lax.broadcasted_iota(jnp.int32, (E, LANES), 1)
    cnt = jnp.zeros((E, LANES), F32)
    ties = jnp.zeros((E, 1), F32)
    slots = jnp.zeros((E, 1), F32)
    nt = Tt // TM
    for t in range(nt):
        if t == 1:
            ties = jnp.zeros((E, 1), F32)
        need = need_ctx if t == 0 else need_lat
        ut = pltpu.bitcast(aff_ref[0, :, t * TM:(t + 1) * TM], jnp.int32)
        tt = v_ctx if t == 0 else v_lat
        eq = jnp.where(ut == tt, 1.0, 0.0)
        rank = ties + jnp.dot(eq.astype(BF16), tri, preferred_element_type=F32)
        sel = jnp.where(ut > tt, 1.0, jnp.where(rank < need, eq, 0.0))
        pos = slots + jnp.dot(sel.astype(BF16), tri, preferred_element_type=F32)
        pos_ref[0, :, t * TM:(t + 1) * TM] = jnp.where(sel > 0.0, pos, -1.0).astype(jnp.int32)
        cnt = jnp.where(lane_t == t, slots, cnt)
        ties = ties + jnp.sum(eq, axis=1, keepdims=True)
        slots = slots + jnp.sum(sel, axis=1, keepdims=True)
    cnt = jnp.where(lane_t == nt, slots, cnt)
    cnt_ref[0] = cnt.astype(jnp.int32)


def expert_choice_slots(aff):
    B, E, Tt = aff.shape
    L = CTX_LEN
    k_ctx = EC_FACTOR * L // N_EXPERTS
    k_lat = EC_FACTOR * (Tt - L) // N_EXPERTS
    assert Tt // TM + 1 <= LANES
    return pl.pallas_call(
        functools.partial(_topk_kernel, L=L, k_ctx=k_ctx, k_lat=k_lat),
        grid=(B,),
        in_specs=[pl.BlockSpec((1, E, Tt), lambda b: (b, 0, 0))],
        out_specs=[pl.BlockSpec((1, E, Tt), lambda b: (b, 0, 0)), pl.BlockSpec((1, E, LANES), lambda b: (b, 0, 0))],
        out_shape=[jax.ShapeDtypeStruct((B, E, Tt), jnp.int32), jax.ShapeDtypeStruct((B, E, LANES), jnp.int32)],
        compiler_params=_cparams(1), name="expert_choice_slots",
    )(aff)


def _ffn_kernel(cnt_ref, h_ref, pos_ref, wg_ref, wu_ref, wd_ref, y_ref, xin_ref, acc_ref, *, ns, nt, unroll):
    b, e, f = pl.program_id(0), pl.program_id(1), pl.program_id(2)
    W = SLOT_WIN

    @pl.when(f == 0)
    def _gather():
        xin_ref[...] = jnp.zeros_like(xin_ref)
        acc_ref[...] = jnp.zeros_like(acc_ref)

        def window_rows(t, base):
            tok0 = pl.multiple_of(t * TM, TM)
            srow = base + lax.broadcasted_iota(jnp.int32, (W, TM), 0)
            onehot = jnp.where(pos_ref[0, 0, :, pl.ds(tok0, TM)] == srow, 1.0, 0.0).astype(BF16)
            return jnp.dot(onehot, h_ref[0, pl.ds(tok0, TM), :], preferred_element_type=F32)

        def add_rows(base, rows):
            xin_ref[pl.ds(base, W), :] = (xin_ref[pl.ds(base, W), :].astype(F32) + rows).astype(BF16)

        def group_body(g, carry):
            tiles = [g * unroll + k for k in range(unroll)]
            bases = [pl.multiple_of((cnt_ref[b, e, t] // 16) * 16, 16) for t in tiles]
            first = [window_rows(t, base) for t, base in zip(tiles, bases)]
            for t, base, rows in zip(tiles, bases, first):
                add_rows(base, rows)

                def more(r, c2, t=t, base=base):
                    nxt = pl.multiple_of(base + r * W, 16)
                    add_rows(nxt, window_rows(t, nxt))
                    return c2

                lax.fori_loop(1, (cnt_ref[b, e, t + 1] - base + W - 1) // W, more, 0)
            return carry

        lax.fori_loop(0, nt // unroll, group_body, 0)

    x = xin_ref[0:ns, :]
    a = jnp.dot(x, wg_ref[0, 0].astype(BF16), preferred_element_type=F32)
    u = jnp.dot(x, wu_ref[0, 0].astype(BF16), preferred_element_type=F32)
    hmid = (a * _sigmoid(a) * u).astype(BF16)
    acc_ref[...] += jnp.dot(hmid, wd_ref[0, 0].astype(BF16), preferred_element_type=F32)

    @pl.when(f == pl.num_programs(2) - 1)
    def _():
        y_ref[0, 0, 0:ns, :] = acc_ref[...].astype(BF16)
        y_ref[0, 0, ns:ns + SLOT_WIN, :] = jnp.zeros((SLOT_WIN, y_ref.shape[-1]), BF16)


def expert_ffn(cnt, h2, pos, w_gate, w_up, w_down, layer):
    B, Tt, D = h2.shape
    _, E, _, F = w_gate.shape
    L = CTX_LEN
    ns = EC_FACTOR * L // N_EXPERTS + EC_FACTOR * (Tt - L) // N_EXPERTS
    assert ns % 16 == 0
    tf = min(512, F)
    nt = Tt // TM
    unroll = max(u for u in range(1, 12) if nt % u == 0)
    grid_spec = pltpu.PrefetchScalarGridSpec(
        num_scalar_prefetch=1,
        grid=(B, E, F // tf),
        in_specs=[pl.BlockSpec((1, Tt, D), lambda b, e, f, c: (b, 0, 0), pipeline_mode=pl.Buffered(1)),
                  pl.BlockSpec((1, 1, 1, Tt), lambda b, e, f, c: (b, e, 0, 0)),
                  pl.BlockSpec((1, 1, D, tf), lambda b, e, f, c: (layer, e, 0, f)),
                  pl.BlockSpec((1, 1, D, tf), lambda b, e, f, c: (layer, e, 0, f)),
                  pl.BlockSpec((1, 1, tf, D), lambda b, e, f, c: (layer, e, f, 0))],
        out_specs=pl.BlockSpec((1, 1, ns + SLOT_WIN, D), lambda b, e, f, c: (b, e, 0, 0)),
        scratch_shapes=[pltpu.VMEM((ns + SLOT_WIN, D), BF16), pltpu.VMEM((ns, D), F32)])
    return pl.pallas_call(
        functools.partial(_ffn_kernel, ns=ns, nt=nt, unroll=unroll),
        grid_spec=grid_spec,
        out_shape=jax.ShapeDtypeStruct((B, E, ns + SLOT_WIN, D), BF16),
        compiler_params=_cparams(3), name="expert_ffn",
    )(cnt, h2, pos.reshape(B, E, 1, Tt), w_gate, w_up, w_down)


def _combine_kernel(cnt_ref, pos_ref, aff_ref, y_ref, x_ref, mod_ref, o_ref, yw_ref, p_ref, acc_ref, *, D, ns, t0):
    b, t = pl.program_id(0), pl.program_id(1) + t0
    E = pos_ref.shape[1]
    W = SLOT_WIN
    starts, rounds = [], jnp.int32(0)
    for e in range(E):
        lo, hi = cnt_ref[b, e, t], cnt_ref[b, e, t + 1]
        s0 = (lo // 16) * 16
        starts.append(s0)
        rounds = jnp.maximum(rounds, jnp.where(hi > lo, (hi - s0 + W - 1) // W, 0))
    acc_ref[...] = jnp.zeros_like(acc_ref)

    def round_body(r, carry):
        for e in range(E):
            base = pl.multiple_of(jnp.minimum(starts[e] + r * W, ns), 16)
            yw_ref[e * W:(e + 1) * W, :] = y_ref[0, e, pl.ds(base, W), :]
            srow = base + lax.broadcasted_iota(jnp.int32, (W, TM), 0)
            p_ref[e * W:(e + 1) * W, :] = jnp.where(pos_ref[0, e:e + 1, :] == srow, aff_ref[0, e:e + 1, :], 0.0).astype(BF16)
        acc_ref[...] += lax.dot_general(p_ref[...], yw_ref[...], TN_DIMS, preferred_element_type=F32)
        return carry

    lax.fori_loop(0, rounds, round_body, 0)
    o_ref[0] = x_ref[0] + mod_ref[0, 0][:, 5 * D:6 * D] * acc_ref[...]


def combine(cnt, pos, aff, Y, X1, modsel, latent_only):
    B, Tt, D = X1.shape
    E = pos.shape[1]
    nsp = Y.shape[2]
    t0 = 1 if latent_only else 0
    grid_spec = pltpu.PrefetchScalarGridSpec(
        num_scalar_prefetch=1,
        grid=(B, Tt // TM - t0),
        in_specs=[pl.BlockSpec((1, E, TM), lambda b, t, c: (b, 0, t + t0)),
                  pl.BlockSpec((1, E, TM), lambda b, t, c: (b, 0, t + t0)),
                  pl.BlockSpec((1, E, nsp, D), lambda b, t, c: (b, 0, 0, 0), pipeline_mode=pl.Buffered(1)),
                  pl.BlockSpec((1, TM, D), lambda b, t, c: (b, t + t0, 0)),
                  pl.BlockSpec((1, 1, 1, 6 * D), lambda b, t, c: (b, jnp.minimum(t + t0, 1), 0, 0))],
        out_specs=pl.BlockSpec((1, TM, D), lambda b, t, c: (b, t, 0)),
        scratch_shapes=[pltpu.VMEM((E * SLOT_WIN, D), BF16), pltpu.VMEM((E * SLOT_WIN, TM), BF16),
                        pltpu.VMEM((TM, D), F32)])
    return pl.pallas_call(
        functools.partial(_combine_kernel, D=D, ns=nsp - SLOT_WIN, t0=t0),
        grid_spec=grid_spec,
        out_shape=jax.ShapeDtypeStruct((B, Tt - t0 * TM, D), F32),
        compiler_params=_cparams(2), name="combine",
    )(cnt, pos, aff, Y, X1, modsel)


def _pack_w_in(w, D):
    widths = (NA_W, NA_W, NA_W, M_W, M_W, M_W, M_W, N_GATE_COLS, 2 * CONV_CH, 3 * D)
    offs = [0]
    for wd in widths:
        offs.append(offs[-1] + wd)
    seg = [w[:, offs[i]:offs[i + 1]] for i in range(len(widths))]
    return jnp.concatenate(seg[:7] + seg[8:], axis=1).astype(BF16), _gate_rows(seg[7].T).astype(BF16)


def _gate_rows(g):
    H = M_HEADS
    i_fw, f_fw, i_bw, f_bw = (g[j * H:(j + 1) * H] for j in range(4))
    return jnp.concatenate([i_fw, f_fw, f_fw, i_fw, i_bw, f_bw, f_bw, i_bw], axis=0)


def _rope_tables(T, L):
    nf = M_HEAD_DIM // 4
    t = jnp.arange(T)
    row = (t // GRID_W).astype(F32)
    col = (t % GRID_W).astype(F32)
    inv = ROPE_BASE ** (-jnp.arange(nf, dtype=F32) / nf)
    ar, ac = row[:, None] * inv[None, :], col[:, None] * inv[None, :]
    z = jnp.zeros((T, nf), F32)
    cos = jnp.concatenate([jnp.cos(ar), jnp.cos(ar), jnp.cos(ac), jnp.cos(ac)], axis=1)
    sina = jnp.concatenate([-jnp.sin(ar), z, -jnp.sin(ac), z], axis=1)
    sinb = jnp.concatenate([z, jnp.sin(ar), z, jnp.sin(ac)], axis=1)
    ident = lambda v: jnp.full((L, M_HEAD_DIM), v, F32)
    return (jnp.concatenate([ident(1.0), cos]), jnp.concatenate([ident(0.0), sina]),
            jnp.concatenate([ident(0.0), sinb]))


def kernel(x, c, ctx, c_ctx, w_mod, b_mod, w_in, na_q_gain, na_k_gain, na_rpb, m_gate_bias,
           m_norm_gain, conv_w, conv_b, conv_ln_g, conv_ln_b, w_br_na, w_br_m, w_br_conv, w_out,
           w_router, w_gate, w_up, w_down):
    B, T, D = x.shape
    L = ctx.shape[1]
    assert L == CTX_LEN == TM and T % TM == 0 and B + 1 <= 16
    depth = w_in.shape[0]
    X = jnp.concatenate([ctx, x], axis=1)
    cc = jnp.zeros((16, D), F32).at[:B].set(c).at[B].set(c_ctx)
    mods = modulation(cc, w_mod, b_mod)
    cos, sina, sinb = _rope_tables(T, L)
    hd = jnp.arange(NA_W) // NA_HEAD_DIM
    gmat = (hd[:, None] == hd[None, :]).astype(BF16)
    bias_tables = na_bias_tables(na_rpb * LOG2E)
    for l in range(depth):
        modsel = jnp.stack([jnp.broadcast_to(mods[l, B], (B, 6 * D)), mods[l, :B]], axis=1).reshape(B, 2, 1, 6 * D)
        q_gain = (jnp.tile(na_q_gain[l], NA_HEADS) * (NA_HEAD_DIM ** -0.5 * LOG2E)).reshape(1, NA_W)
        k_gain = jnp.tile(na_k_gain[l], NA_HEADS).reshape(1, NA_W)
        w_tok, w_gates_t = _pack_w_in(w_in[l], D)
        naq, nak, nav, mq, mk, mv, mo, mgf, mgb, cu, bg = input_projection(
            X, modsel, w_tok, q_gain, k_gain, gmat, cos, sina, sinb, w_gates_t,
            _gate_rows(m_gate_bias[l].reshape(N_GATE_COLS, 1)))
        na = na_attention(naq, nak, nav, bias_tables[l])
        hfw, hbw = mlstm(mq, mk, mv, mgf, mgb)
        X1, h2, aff = merge_and_route(
            na, hfw, hbw, mo, cu, bg, X, modsel, m_norm_gain[l].reshape(1, M_W),
            conv_w[l], conv_b[l], conv_ln_g[l], conv_ln_b[l],
            w_br_na[l].astype(BF16), w_br_m[l].astype(BF16), w_br_conv[l].astype(BF16), w_out[l].astype(BF16),
            w_router[l].T.astype(BF16))
        pos, cnt = expert_choice_slots(aff)
        cnt = cnt[:, :, :T // TM + 2]
        Y = expert_ffn(cnt, h2, pos, w_gate, w_up, w_down, l)
        X = combine(cnt, pos, aff, Y, X1, modsel, latent_only=(l == depth - 1))
    return X
```

```python
import functools

import jax
import jax.numpy as jnp
from jax import lax
from jax.experimental import pallas as pl
from jax.experimental.pallas import tpu as pltpu

DEPTH = 4
CTX_LEN = 256
GRID_W = 64
NA_HEADS = 8
NA_HEAD_DIM = 64
NA_KH = 8
NA_KW = 16
M_HEADS = 4
M_HEAD_DIM = 128
CONV_CH = 512
CONV_K = 31
N_EXPERTS = 16
EC_FACTOR = 2
ROPE_BASE = 10000.0
NORM_EPS = 1e-6

NA_W = NA_HEADS * NA_HEAD_DIM
M_W = M_HEADS * M_HEAD_DIM
N_GATE_COLS = 4 * M_HEADS

LANES = 128
TM = 256
M_CH = 128
CONV_HALO = 16
CONV_ROWS = 64
SLOT_WIN = 64
NEG = -1e30
LOG2E = 1.4426950408889634
VMEM_LIMIT_BYTES = 56 * 1024 * 1024

F32 = jnp.float32
BF16 = jnp.bfloat16
NT_DIMS = (((1,), (1,)), ((), ()))
TN_DIMS = (((0,), (0,)), ((), ()))


def _cparams(n_axes):
    return pltpu.CompilerParams(dimension_semantics=("arbitrary",) * n_axes,
                                vmem_limit_bytes=VMEM_LIMIT_BYTES)


def _sigmoid(x):
    return 1.0 / (1.0 + jnp.exp(-x))


def _modnorm(x, shift, scale):
    ms = jnp.mean(x * x, axis=-1, keepdims=True)
    return (x * lax.rsqrt(ms + NORM_EPS)) * (1.0 + scale) + shift


def _chunk(width):
    return 512 if width % 512 == 0 else 256


def _mod_kernel(c_ref, w_ref, b_ref, o_ref):
    c = c_ref[...]
    s = (c * _sigmoid(c)).astype(BF16)
    o_ref[0] = jnp.dot(s, w_ref[0].astype(BF16), preferred_element_type=F32) + b_ref[0]


def modulation(cc, w_mod, b_mod):
    depth, D, N = w_mod.shape
    tn = 1024 if N % 1024 == 0 else N
    return pl.pallas_call(
        _mod_kernel,
        grid=(depth, N // tn),
        in_specs=[pl.BlockSpec((16, D), lambda l, n: (0, 0)),
                  pl.BlockSpec((1, D, tn), lambda l, n: (l, 0, n)),
                  pl.BlockSpec((1, 1, tn), lambda l, n: (l, 0, n))],
        out_specs=pl.BlockSpec((1, 16, tn), lambda l, n: (l, 0, n)),
        out_shape=jax.ShapeDtypeStruct((depth, 16, N), F32),
        compiler_params=_cparams(2), name="modulation",
    )(cc, w_mod, b_mod.reshape(depth, 1, N))


def _inproj_kernel(x_ref, mod_ref, w_ref, qg_ref, kg_ref, gm_ref, cos_ref, sa_ref, sb_ref, wgt_ref, gb_ref,
                   naq_ref, nak_ref, nav_ref, mq_ref, mk_ref, mv_ref, mo_ref, mgf_ref, mgb_ref, cu_ref, bg_ref, *, D):
    mod = mod_ref[0, 0]
    h = _modnorm(x_ref[0], mod[:, 0:D], mod[:, D:2 * D]).astype(BF16)

    def proj(off, width):
        return jnp.dot(h, w_ref[:, off:off + width], preferred_element_type=F32)

    gm = gm_ref[...]
    for ref, gain_ref, off in ((naq_ref, qg_ref, 0), (nak_ref, kg_ref, NA_W)):
        y = proj(off, NA_W)
        ysq = y * y
        hi = ysq.astype(BF16)
        lo = (ysq - hi.astype(F32)).astype(BF16)
        ss = jnp.dot(hi, gm, preferred_element_type=F32) + jnp.dot(lo, gm, preferred_element_type=F32)
        ref[0] = (y * lax.rsqrt(ss * (1.0 / NA_HEAD_DIM) + NORM_EPS) * gain_ref[...]).astype(BF16)
    nav_ref[0] = proj(2 * NA_W, NA_W).astype(BF16)

    cos, sa, sb = cos_ref[...], sa_ref[...], sb_ref[...]
    q4 = M_HEAD_DIM // 4
    base = 3 * NA_W
    for ref, off, scale in ((mq_ref, base, 1.0), (mk_ref, base + M_W, M_HEAD_DIM ** -0.5)):
        y = proj(off, M_W)
        for hh in range(M_HEADS):
            yh = y[:, hh * M_HEAD_DIM:(hh + 1) * M_HEAD_DIM]
            r = yh * cos + pltpu.roll(yh, M_HEAD_DIM - q4, 1) * sa + pltpu.roll(yh, q4, 1) * sb
            ref[0, :, hh * M_HEAD_DIM:(hh + 1) * M_HEAD_DIM] = (r * scale).astype(BF16)
    mv_ref[0] = proj(base + 2 * M_W, M_W).astype(BF16)
    mo_ref[0] = proj(base + 3 * M_W, M_W).astype(BF16)
    base += 4 * M_W
    for ref, width in ((cu_ref, 2 * CONV_CH), (bg_ref, 3 * D)):
        cw = _chunk(width)
        for j in range(width // cw):
            ref[0, :, j * cw:(j + 1) * cw] = proj(base + j * cw, cw).astype(BF16)
        base += width
    g = lax.dot_general(wgt_ref[...], h, NT_DIMS, preferred_element_type=F32) + gb_ref[...]
    nrow = g.shape[0] // 2
    for cidx in range(g.shape[1] // M_CH):
        mgf_ref[0, cidx] = g[0:nrow, cidx * M_CH:(cidx + 1) * M_CH]
        mgb_ref[0, cidx] = g[nrow:2 * nrow, cidx * M_CH:(cidx + 1) * M_CH]


def input_projection(X, modsel, w_packed, q_gain, k_gain, gmat, cos, sina, sinb, w_gates_t, gate_bias):
    B, Tt, D = X.shape
    NP = w_packed.shape[1]
    nt = Tt // TM
    tok = lambda w: pl.BlockSpec((1, TM, w), lambda b, t: (b, t, 0))
    const = lambda shape: pl.BlockSpec(shape, lambda b, t: (0,) * len(shape))
    out_widths = (NA_W, NA_W, NA_W, M_W, M_W, M_W, M_W, None, None, 2 * CONV_CH, 3 * D)
    gate_rows = w_gates_t.shape[0] // 2
    gate_spec = pl.BlockSpec((1, TM // M_CH, gate_rows, M_CH), lambda b, t: (b, t, 0, 0))
    return pl.pallas_call(
        functools.partial(_inproj_kernel, D=D),
        grid=(B, nt),
        in_specs=[tok(D),
                  pl.BlockSpec((1, 1, 1, 6 * D), lambda b, t: (b, jnp.minimum(t, 1), 0, 0)),
                  pl.BlockSpec((D, NP), lambda b, t: (0, 0), pipeline_mode=pl.Buffered(1)),
                  const((1, NA_W)), const((1, NA_W)), const((NA_W, NA_W)),
                  pl.BlockSpec((TM, M_HEAD_DIM), lambda b, t: (t, 0)),
                  pl.BlockSpec((TM, M_HEAD_DIM), lambda b, t: (t, 0)),
                  pl.BlockSpec((TM, M_HEAD_DIM), lambda b, t: (t, 0)),
                  const((2 * gate_rows, D)), const((2 * gate_rows, 1))],
        out_specs=[gate_spec if w is None else tok(w) for w in out_widths],
        out_shape=[jax.ShapeDtypeStruct((B, Tt // M_CH, gate_rows, M_CH), F32) if w is None
                   else jax.ShapeDtypeStruct((B, Tt, w), BF16) for w in out_widths],
        compiler_params=_cparams(2), name="input_projection",
    )(X, modsel, w_packed, q_gain, k_gain, gmat, cos, sina, sinb, w_gates_t, gate_bias)


def _na_kernel(q_ref, k_ref, v_ref, bias_ref, o_ref, *, L, rows):
    t = pl.program_id(2)
    even = lax.broadcasted_iota(jnp.int32, (1, LANES), 1) < NA_HEAD_DIM
    pairs = [slice(jj * LANES, (jj + 1) * LANES) for jj in range(NA_PAIRS)]

    def heads_stack(q):
        z = jnp.zeros_like(q)
        return jnp.concatenate([jnp.where(even, q, z), jnp.where(even, z, q)], axis=0)

    def heads_merge(o):
        n = o.shape[0] // 2
        return jnp.where(even, o[:n], o[n:])

    @pl.when(t == 0)
    def _context_queries():
        for sl in pairs:
            kc, vc = k_ref[0, 0:L, sl], v_ref[0, 0:L, sl]
            q2 = heads_stack(q_ref[0, :, sl])
            s = lax.dot_general(q2, kc, NT_DIMS, preferred_element_type=F32)
            m = jnp.max(s, axis=-1, keepdims=True)
            p = jnp.exp2(s - m)
            l = jnp.sum(p, axis=-1, keepdims=True)
            o = jnp.dot(p.astype(BF16), vc, preferred_element_type=F32) / l
            o_ref[0, :, sl] = heads_merge(o).astype(BF16)

    @pl.when(t > 0)
    def _latent_queries():
        for rr in range(TM // GRID_W):
            r = (t - 1) * (TM // GRID_W) + rr
            rs = jnp.clip(r - NA_KH // 2, 0, rows - NA_KH)
            start = pl.multiple_of(L + rs * GRID_W, GRID_W)
            qrows = slice(rr * GRID_W, (rr + 1) * GRID_W)
            for jj, sl in enumerate(pairs):
                kc, vc = k_ref[0, 0:L, sl], v_ref[0, 0:L, sl]
                kw = k_ref[0, pl.ds(start, NA_KH * GRID_W), sl]
                vw = v_ref[0, pl.ds(start, NA_KH * GRID_W), sl]
                q2 = heads_stack(q_ref[0, qrows, sl])
                s_lat = lax.dot_general(q2, kw, NT_DIMS, preferred_element_type=F32) + bias_ref[jj, r - rs]
                s_ctx = lax.dot_general(q2, kc, NT_DIMS, preferred_element_type=F32)
                m = jnp.maximum(jnp.max(s_lat, axis=-1, keepdims=True), jnp.max(s_ctx, axis=-1, keepdims=True))
                p_lat = jnp.exp2(s_lat - m)
                p_ctx = jnp.exp2(s_ctx - m)
                l = jnp.sum(p_lat, axis=-1, keepdims=True) + jnp.sum(p_ctx, axis=-1, keepdims=True)
                o = (jnp.dot(p_lat.astype(BF16), vw, preferred_element_type=F32)
                     + jnp.dot(p_ctx.astype(BF16), vc, preferred_element_type=F32)) / l
                o_ref[0, qrows, sl] = heads_merge(o).astype(BF16)


NA_PAIRS = 4


def na_attention(q, k, v, bias):
    B, Tt, _ = q.shape
    L = CTX_LEN
    rows = (Tt - L) // GRID_W
    w = NA_PAIRS * LANES
    return pl.pallas_call(
        functools.partial(_na_kernel, L=L, rows=rows),
        grid=(B, NA_W // w, Tt // TM),
        in_specs=[pl.BlockSpec((1, TM, w), lambda b, j, t: (b, t, j)),
                  pl.BlockSpec((1, Tt, w), lambda b, j, t: (b, 0, j)),
                  pl.BlockSpec((1, Tt, w), lambda b, j, t: (b, 0, j)),
                  pl.BlockSpec((NA_PAIRS,) + bias.shape[1:], lambda b, j, t: (j, 0, 0, 0),
                               pipeline_mode=pl.Buffered(1))],
        out_specs=pl.BlockSpec((1, TM, w), lambda b, j, t: (b, t, j)),
        out_shape=jax.ShapeDtypeStruct((B, Tt, NA_W), BF16),
        compiler_params=_cparams(3), name="na_attention",
    )(q, k, v, bias)


def na_bias_tables(rpb):
    depth, H = rpb.shape[:2]
    cq = jnp.arange(GRID_W)
    col_start = jnp.clip(cq - NA_KW // 2, 0, GRID_W - NA_KW)
    col_mask = (cq[None, :] >= col_start[:, None]) & (cq[None, :] < col_start[:, None] + NA_KW)
    col_idx = jnp.clip(cq[None, :] - cq[:, None] + NA_KW - 1, 0, 2 * NA_KW - 2)
    pick = (col_idx[None] == jnp.arange(2 * NA_KW - 1)[:, None, None]).astype(F32)
    cols = jnp.einsum('lhrc,cqk->lhrqk', rpb, pick, precision=lax.Precision.HIGHEST)
    cols = jnp.where(col_mask, cols, NEG)
    tbl = jnp.stack([cols[:, :, NA_KH - 1 - dl:2 * NA_KH - 1 - dl] for dl in range(NA_KH)], axis=2)
    tbl = jnp.transpose(tbl, (0, 1, 2, 4, 3, 5)).reshape(depth, H // 2, 2, NA_KH, GRID_W, NA_KH * GRID_W)
    return jnp.transpose(tbl, (0, 1, 3, 2, 4, 5)).reshape(depth, H // 2, NA_KH, 2 * GRID_W, NA_KH * GRID_W)


def _log_sigmoid(x):
    return jnp.minimum(x, 0.0) - jnp.log(1.0 + jnp.exp(-jnp.abs(x)))


def _mlstm_gate_kernel(gf_ref, gb_ref, rqf_ref, rqb_ref, col_ref, t_scr):
    nc, n = gf_ref.shape[1], M_CH
    rows = nc * 8
    lane = lax.broadcasted_iota(jnp.int32, (rows, n), 1)
    for d, (g_ref, rq_ref) in enumerate(((gf_ref, rqf_ref), (gb_ref, rqb_ref))):
        def scan(x, op, fill):
            sh = 1
            while sh < n:
                if d == 0:
                    x = op(x, jnp.where(lane >= sh, pltpu.roll(x, sh, 1), fill))
                else:
                    x = op(x, jnp.where(lane < n - sh, pltpu.roll(x, n - sh, 1), fill))
                sh *= 2
            return x

        top = g_ref[0, :, 0:8, :].reshape(rows, n)
        bot = g_ref[0, :, 8:16, :].reshape(rows, n)
        b = scan(_log_sigmoid(bot), jnp.add, 0.0)
        last = n - 1 if d == 0 else 0
        tot = jnp.broadcast_to(b[:, last:last + 1], (rows, n))
        r = top - b
        cm = scan(r, jnp.maximum, NEG)
        amax = jnp.broadcast_to(jnp.max(tot + r, axis=1, keepdims=True), (rows, n))
        for j, val in enumerate((r, cm, tot, amax)):
            rq_ref[0, :, 8 * j:8 * (j + 1), :] = val.reshape(nc, 8, n)
        t_scr[:, 16 * d:16 * d + 8, :] = b.reshape(nc, 8, n)
        t_scr[:, 16 * d + 8:16 * d + 16, :] = cm.reshape(nc, 8, n)

    def transpose_chunk(ci, carry):
        tile = jnp.concatenate([t_scr[ci], jnp.zeros((n - 32, n), F32)], axis=0)
        col_ref[0, ci] = tile.T
        return carry

    lax.fori_loop(0, nc, transpose_chunk, 0)


def mlstm_gates(gf, gb):
    B, nc, _, n = gf.shape
    blk = lambda r: pl.BlockSpec((1, nc, r, n), lambda b: (b, 0, 0, 0))
    return pl.pallas_call(
        _mlstm_gate_kernel,
        grid=(B,),
        in_specs=[blk(16), blk(16)],
        out_specs=[blk(32), blk(32), blk(n)],
        out_shape=[jax.ShapeDtypeStruct((B, nc, 32, n), F32)] * 2 + [jax.ShapeDtypeStruct((B, nc, n, n), F32)],
        scratch_shapes=[pltpu.VMEM((nc, 32, n), F32)],
        compiler_params=_cparams(1), name="mlstm_gates",
    )(gf, gb)


def _mlstm_chain_group(d, q_ref, k_ref, v_ref, rq_ref, col_ref, o_ref, ct_ref, m_ref, bb):
    n, H = M_CH, M_HEADS
    row8 = lax.broadcasted_iota(jnp.int32, (2 * H, n), 0)
    lane = lax.broadcasted_iota(jnp.int32, (n, n), 1)
    sub = lax.broadcasted_iota(jnp.int32, (n, n), 0)
    eye = lane == sub
    causal = (lane <= sub) if d == 0 else (lane >= sub)
    one0 = jnp.where(lane == 0, 1.0, 0.0).astype(BF16)

    r8, cm8 = rq_ref[bb, 0, 0:8, :], rq_ref[bb, 0, 8:16, :]
    tot, amax = rq_ref[bb, 0, 16:24, :], rq_ref[bb, 0, 24:32, :]
    m0 = m_ref[d, bb]
    nm8 = jnp.maximum(m0, cm8)
    m_new = jnp.maximum(tot + m0, amax)
    decay8 = jnp.exp(tot + m0 - m_new)
    w8 = jnp.exp(tot + r8 - m_new)
    e8 = jnp.exp(m0 - nm8)
    m_ref[d, bb] = jnp.where(row8 < H, m_new, 0.0)
    tc = col_ref[bb, 0]

    for hh in range(H):
        sl = slice(hh * M_HEAD_DIM, (hh + 1) * M_HEAD_DIM)
        qh, kh, vh = q_ref[bb, :, sl], k_ref[bb, :, sl], v_ref[bb, :, sl]
        b_col = tc[:, 16 * d + hh:16 * d + hh + 1]
        nm_col = jnp.maximum(m0[hh:hh + 1, 0:1], tc[:, 16 * d + 8 + hh:16 * d + 8 + hh + 1])
        dmat = jnp.where(causal, jnp.exp(r8[hh:hh + 1, :] - nm_col), 0.0)
        s = (lax.dot_general(qh, kh, NT_DIMS, preferred_element_type=F32) * dmat).astype(BF16)
        e_diag = jnp.where(eye, e8[hh:hh + 1, :], 0.0).astype(BF16)
        w_diag = jnp.where(eye, w8[hh:hh + 1, :], 0.0).astype(BF16)
        vaug = jnp.concatenate([vh, one0], axis=1)
        ct = ct_ref[d, bb, hh]
        inter = jnp.dot(qh, ct.astype(BF16), preferred_element_type=F32).astype(BF16)
        num = jnp.dot(jnp.concatenate([s, e_diag], axis=1), jnp.concatenate([vaug, inter], axis=0),
                      preferred_element_type=F32)
        den = num[:, M_HEAD_DIM:M_HEAD_DIM + 1]
        o_ref[bb, :, sl] = num[:, :M_HEAD_DIM] / jnp.maximum(jnp.abs(den), jnp.exp(-(b_col + nm_col)))
        wv = jnp.dot(w_diag, vaug, preferred_element_type=F32).astype(BF16)
        ct_ref[d, bb, hh] = decay8[hh:hh + 1, 0:1] * ct + lax.dot_general(kh, wv, TN_DIMS, preferred_element_type=F32)


def _mlstm_kernel(qf_ref, kf_ref, vf_ref, rqf_ref, colf_ref, qb_ref, kb_ref, vb_ref, rqb_ref, colb_ref,
                  of_ref, ob_ref, ct_ref, m_ref):
    @pl.when(pl.program_id(1) == 0)
    def _():
        ct_ref[...] = jnp.zeros_like(ct_ref)
        m_ref[...] = jnp.zeros_like(m_ref)

    for bb in range(qf_ref.shape[0]):
        _mlstm_chain_group(0, qf_ref, kf_ref, vf_ref, rqf_ref, colf_ref, of_ref, ct_ref, m_ref, bb)
        _mlstm_chain_group(1, qb_ref, kb_ref, vb_ref, rqb_ref, colb_ref, ob_ref, ct_ref, m_ref, bb)


M_BATCH = 4


def mlstm(q, k, v, gf, gb):
    B, Tt, _ = q.shape
    nc = Tt // M_CH
    lc = CTX_LEN // M_CH
    nb = M_BATCH if B % M_BATCH == 0 else 1
    rqf, rqb, col = mlstm_gates(gf, gb)

    def bw(s):
        return jnp.where(s < lc, lc - 1 - s, nc - 1 + lc - s)

    tok_f = pl.BlockSpec((nb, M_CH, M_W), lambda b, s: (b, s, 0))
    tok_b = pl.BlockSpec((nb, M_CH, M_W), lambda b, s: (b, bw(s), 0))
    chunk_f = lambda r: pl.BlockSpec((nb, 1, r, M_CH), lambda b, s: (b, s, 0, 0))
    chunk_b = lambda r: pl.BlockSpec((nb, 1, r, M_CH), lambda b, s: (b, bw(s), 0, 0))
    return pl.pallas_call(
        _mlstm_kernel,
        grid=(B // nb, nc),
        in_specs=[tok_f, tok_f, tok_f, chunk_f(32), chunk_f(M_CH), tok_b, tok_b, tok_b, chunk_b(32), chunk_b(M_CH)],
        out_specs=[tok_f, tok_b],
        out_shape=[jax.ShapeDtypeStruct((B, Tt, M_W), F32)] * 2,
        scratch_shapes=[pltpu.VMEM((2, nb, M_HEADS, M_HEAD_DIM, M_HEAD_DIM + LANES), F32),
                        pltpu.VMEM((2, nb, 2 * M_HEADS, M_CH), F32)],
        compiler_params=_cparams(2), name="mlstm",
    )(q, k, v, rqf, col, q, k, v, rqb, col)


def _conv_tile(t, nt, cur_ref, prev_ref, next_ref, w_ref, b_ref, g_ref, be_ref, o_ref, hbuf, hsh):
    C = CONV_CH

    def glu(u):
        u = u.astype(F32)
        return u[:, :C] * _sigmoid(u[:, C:])

    prev_ok = t >= 2
    next_ok = jnp.logical_and(t >= 1, t < nt - 1)
    hbuf[0:CONV_HALO] = jnp.where(prev_ok, glu(prev_ref[0]), 0.0)
    hbuf[CONV_HALO:CONV_HALO + TM] = glu(cur_ref[0])
    hbuf[CONV_HALO + TM:2 * CONV_HALO + TM] = jnp.where(next_ok, glu(next_ref[0]), 0.0)
    span = TM + 2 * CONV_HALO - 8
    for r in range(8):
        hsh[r] = hbuf[r:r + span, :]
    off = CONV_HALO - CONV_K // 2
    for rc in range(TM // CONV_ROWS):
        acc = jnp.zeros((CONV_ROWS, C), F32) + b_ref[...]
        for j in range(CONV_K):
            a, r = divmod(off + j, 8)
            acc = acc + hsh[r, rc * CONV_ROWS + 8 * a:rc * CONV_ROWS + 8 * a + CONV_ROWS, :] * w_ref[j:j + 1, :]
        mu = jnp.mean(acc, axis=-1, keepdims=True)
        xc = acc - mu
        var = jnp.mean(xc * xc, axis=-1, keepdims=True)
        y = xc * lax.rsqrt(var + NORM_EPS) * g_ref[...] + be_ref[...]
        o_ref[rc * CONV_ROWS:(rc + 1) * CONV_ROWS, :] = (y * _sigmoid(y)).astype(BF16)


def _merge_kernel(na_ref, hfw_ref, hbw_ref, mo_ref, cu_ref, cup_ref, cun_ref, bg_ref, x_ref, mod_ref, ng_ref,
                  cw_ref, cb_ref, cg_ref, cbe_ref, wna_ref, wm_ref, wc_ref, wo_ref, wr_ref,
                  x1_ref, h2_ref, aff_ref, cv_ref, hbuf, hsh, *, D, nt):
    _conv_tile(pl.program_id(1), nt, cu_ref, cup_ref, cun_ref, cw_ref, cb_ref, cg_ref, cbe_ref, cv_ref, hbuf, hsh)
    mod = mod_ref[0, 0]
    mh = hfw_ref[0] + hbw_ref[0]
    parts = []
    for hh in range(M_HEADS):
        xh = mh[:, hh * M_HEAD_DIM:(hh + 1) * M_HEAD_DIM]
        parts.append(xh * lax.rsqrt(jnp.mean(xh * xh, axis=-1, keepdims=True) + NORM_EPS))
    mn = (jnp.concatenate(parts, axis=1) * ng_ref[...] * _sigmoid(mo_ref[0].astype(F32))).astype(BF16)
    z = (_sigmoid(bg_ref[0, :, 0:D].astype(F32)) * jnp.dot(na_ref[0], wna_ref[...], preferred_element_type=F32)
         + _sigmoid(bg_ref[0, :, D:2 * D].astype(F32)) * jnp.dot(mn, wm_ref[...], preferred_element_type=F32)
         + _sigmoid(bg_ref[0, :, 2 * D:3 * D].astype(F32)) * jnp.dot(cv_ref[...], wc_ref[...], preferred_element_type=F32))
    y = jnp.dot(z.astype(BF16), wo_ref[...], preferred_element_type=F32)
    x1 = x_ref[0] + mod[:, 2 * D:3 * D] * y
    x1_ref[0] = x1
    h2 = _modnorm(x1, mod[:, 3 * D:4 * D], mod[:, 4 * D:5 * D]).astype(BF16)
    h2_ref[0] = h2
    logits = lax.dot_general(wr_ref[...], h2, NT_DIMS, preferred_element_type=F32)
    e = jnp.exp(logits - jnp.max(logits, axis=0, keepdims=True))
    aff_ref[0] = e / jnp.sum(e, axis=0, keepdims=True)


def merge_and_route(na, hfw, hbw, mo, cu, bg, X, modsel, norm_gain, conv_w, conv_b, ln_g, ln_b,
                    w_na, w_m, w_c, w_o, w_rt):
    B, Tt, D = X.shape
    E = w_rt.shape[0]
    C = CONV_CH
    nt = Tt // TM
    hb = TM // CONV_HALO
    nhb = Tt // CONV_HALO
    wpad = jnp.zeros((32, C), F32).at[:CONV_K].set(conv_w)
    vec = lambda a: a.reshape(1, C)
    tok = lambda w: pl.BlockSpec((1, TM, w), lambda b, t: (b, t, 0))
    const = lambda shape: pl.BlockSpec(shape, lambda b, t: (0,) * len(shape))
    return pl.pallas_call(
        functools.partial(_merge_kernel, D=D, nt=nt),
        grid=(B, nt),
        in_specs=[tok(NA_W), tok(M_W), tok(M_W), tok(M_W), tok(2 * C),
                  pl.BlockSpec((1, CONV_HALO, 2 * C), lambda b, t: (b, jnp.maximum(t * hb - 1, 0), 0)),
                  pl.BlockSpec((1, CONV_HALO, 2 * C), lambda b, t: (b, jnp.minimum((t + 1) * hb, nhb - 1), 0)),
                  tok(3 * D), tok(D),
                  pl.BlockSpec((1, 1, 1, 6 * D), lambda b, t: (b, jnp.minimum(t, 1), 0, 0)),
                  const((1, M_W)), const((32, C)), const((1, C)), const((1, C)), const((1, C)),
                  const((NA_W, D)), const((M_W, D)), const((C, D)), const((D, D)), const((E, D))],
        out_specs=[tok(D), tok(D), pl.BlockSpec((1, E, TM), lambda b, t: (b, 0, t))],
        out_shape=[jax.ShapeDtypeStruct((B, Tt, D), F32), jax.ShapeDtypeStruct((B, Tt, D), BF16),
                   jax.ShapeDtypeStruct((B, E, Tt), F32)],
        scratch_shapes=[pltpu.VMEM((TM, C), BF16), pltpu.VMEM((TM + 2 * CONV_HALO, C), F32),
                        pltpu.VMEM((8, TM + 2 * CONV_HALO - 8, C), F32)],
        compiler_params=_cparams(2), name="merge_and_route",
    )(na, hfw, hbw, mo, cu, cu, cu, bg, X, modsel, norm_gain, wpad, vec(conv_b), vec(ln_g), vec(ln_b),
      w_na, w_m, w_c, w_o, w_rt)


def _topk_kernel(aff_ref, pos_ref, cnt_ref, *, L, k_ctx, k_lat):
    E, Tt = aff_ref.shape[1:]
    u = pltpu.bitcast(aff_ref[0], jnp.int32)
    is_ctx = lax.broadcasted_iota(jnp.int32, (E, Tt), 1) < L

    def count(mask):
        f = jnp.where(mask, 1.0, 0.0)
        c_ctx = jnp.sum(jnp.where(is_ctx, f, 0.0), axis=1, keepdims=True)
        return c_ctx, jnp.sum(f, axis=1, keepdims=True) - c_ctx

    def bit_step(i, carry):
        v_ctx, v_lat = carry
        bit = jnp.left_shift(jnp.int32(1), 30 - i)
        c_ctx, c_lat = count(u >= jnp.where(is_ctx, v_ctx | bit, v_lat | bit))
        return (jnp.where(c_ctx >= k_ctx, v_ctx | bit, v_ctx), jnp.where(c_lat >= k_lat, v_lat | bit, v_lat))

    zero = jnp.zeros((E, 1), jnp.int32)
    v_ctx, v_lat = lax.fori_loop(0, 31, bit_step, (zero, zero))
    thr = jnp.where(is_ctx, v_ctx, v_lat)
    g_ctx, g_lat = count(u > thr)
    need_ctx, need_lat = k_ctx - g_ctx, k_lat - g_lat

    r = lax.broadcasted_iota(jnp.int32, (TM, TM), 0)
    c = lax.broadcasted_iota(jnp.int32, (TM, TM), 1)
    tri = jnp.where(r < c, 1.0, 0.0).astype(BF16)
    lane_t = lax.broadcasted_iota(jnp.int32, (E, LANES), 1)
    cnt = jnp.zeros((E, LANES), F32)
    ties = jnp.zeros((E, 1), F32)
    slots = jnp.zeros((E, 1), F32)
    nt = Tt // TM
    for t in range(nt):
        if t == 1:
            ties = jnp.zeros((E, 1), F32)
        need = need_ctx if t == 0 else need_lat
        ut = pltpu.bitcast(aff_ref[0, :, t * TM:(t + 1) * TM], jnp.int32)
        tt = v_ctx if t == 0 else v_lat
        eq = jnp.where(ut == tt, 1.0, 0.0)
        rank = ties + jnp.dot(eq.astype(BF16), tri, preferred_element_type=F32)
        sel = jnp.where(ut > tt, 1.0, jnp.where(rank < need, eq, 0.0))
        pos = slots + jnp.dot(sel.astype(BF16), tri, preferred_element_type=F32)
        pos_ref[0, :, t * TM:(t + 1) * TM] = jnp.where(sel > 0.0, pos, -1.0).astype(jnp.int32)
        cnt = jnp.where(lane_t == t, slots, cnt)
        ties = ties + jnp.sum(eq, axis=1, keepdims=True)
        slots = slots + jnp.sum(sel, axis=1, keepdims=True)
    cnt = jnp.where(lane_t == nt, slots, cnt)
    cnt_ref[0] = cnt.astype(jnp.int32)


def expert_choice_slots(aff):
    B, E, Tt = aff.shape
    L = CTX_LEN
    k_ctx = EC_FACTOR * L // N_EXPERTS
    k_lat = EC_FACTOR * (Tt - L) // N_EXPERTS
    assert Tt // TM + 1 <= LANES
    return pl.pallas_call(
        functools.partial(_topk_kernel, L=L, k_ctx=k_ctx, k_lat=k_lat),
        grid=(B,),
        in_specs=[pl.BlockSpec((1, E, Tt), lambda b: (b, 0, 0))],
        out_specs=[pl.BlockSpec((1, E, Tt), lambda b: (b, 0, 0)), pl.BlockSpec((1, E, LANES), lambda b: (b, 0, 0))],
        out_shape=[jax.ShapeDtypeStruct((B, E, Tt), jnp.int32), jax.ShapeDtypeStruct((B, E, LANES), jnp.int32)],
        compiler_params=_cparams(1), name="expert_choice_slots",
    )(aff)


def _ffn_kernel(cnt_ref, h_ref, pos_ref, wg_ref, wu_ref, wd_ref, y_ref, xin_ref, acc_ref, *, ns, nt, unroll):
    b, e, f = pl.program_id(0), pl.program_id(1), pl.program_id(2)
    W = SLOT_WIN

    @pl.when(f == 0)
    def _gather():
        xin_ref[...] = jnp.zeros_like(xin_ref)
        acc_ref[...] = jnp.zeros_like(acc_ref)

        def window_rows(t, base):
            tok0 = pl.multiple_of(t * TM, TM)
            srow = base + lax.broadcasted_iota(jnp.int32, (W, TM), 0)
            onehot = jnp.where(pos_ref[0, 0, :, pl.ds(tok0, TM)] == srow, 1.0, 0.0).astype(BF16)
            return jnp.dot(onehot, h_ref[0, pl.ds(tok0, TM), :], preferred_element_type=F32)

        def add_rows(base, rows):
            xin_ref[pl.ds(base, W), :] = xin_ref[pl.ds(base, W), :] + rows.astype(BF16)

        def group_body(g, carry):
            tiles = [g * unroll + k for k in range(unroll)]
            bases = [pl.multiple_of((cnt_ref[b, e, t] // 16) * 16, 16) for t in tiles]
            first = [window_rows(t, base) for t, base in zip(tiles, bases)]
            for t, base, rows in zip(tiles, bases, first):
                add_rows(base, rows)

                def more(r, c2, t=t, base=base):
                    nxt = pl.multiple_of(base + r * W, 16)
                    add_rows(nxt, window_rows(t, nxt))
                    return c2

                lax.fori_loop(1, (cnt_ref[b, e, t + 1] - base + W - 1) // W, more, 0)
            return carry

        lax.fori_loop(0, nt // unroll, group_body, 0)

    x = xin_ref[0:ns, :]
    a = jnp.dot(x, wg_ref[0, 0].astype(BF16), preferred_element_type=F32)
    u = jnp.dot(x, wu_ref[0, 0].astype(BF16), preferred_element_type=F32)
    hmid = (a * _sigmoid(a) * u).astype(BF16)
    acc_ref[...] += jnp.dot(hmid, wd_ref[0, 0].astype(BF16), preferred_element_type=F32)

    @pl.when(f == pl.num_programs(2) - 1)
    def _():
        y_ref[0, 0, 0:ns, :] = acc_ref[...].astype(BF16)
        y_ref[0, 0, ns:ns + SLOT_WIN, :] = jnp.zeros((SLOT_WIN, y_ref.shape[-1]), BF16)


def expert_ffn(cnt, h2, pos, w_gate, w_up, w_down, layer):
    B, Tt, D = h2.shape
    _, E, _, F = w_gate.shape
    L = CTX_LEN
    ns = EC_FACTOR * L // N_EXPERTS + EC_FACTOR * (Tt - L) // N_EXPERTS
    assert ns % 16 == 0
    tf = min(512, F)
    nt = Tt // TM
    unroll = max(u for u in range(1, 12) if nt % u == 0)
    grid_spec = pltpu.PrefetchScalarGridSpec(
        num_scalar_prefetch=1,
        grid=(B, E, F // tf),
        in_specs=[pl.BlockSpec((1, Tt, D), lambda b, e, f, c: (b, 0, 0), pipeline_mode=pl.Buffered(1)),
                  pl.BlockSpec((1, 1, 1, Tt), lambda b, e, f, c: (b, e, 0, 0)),
                  pl.BlockSpec((1, 1, D, tf), lambda b, e, f, c: (layer, e, 0, f)),
                  pl.BlockSpec((1, 1, D, tf), lambda b, e, f, c: (layer, e, 0, f)),
                  pl.BlockSpec((1, 1, tf, D), lambda b, e, f, c: (layer, e, f, 0))],
        out_specs=pl.BlockSpec((1, 1, ns + SLOT_WIN, D), lambda b, e, f, c: (b, e, 0, 0)),
        scratch_shapes=[pltpu.VMEM((ns + SLOT_WIN, D), BF16), pltpu.VMEM((ns, D), F32)])
    return pl.pallas_call(
        functools.partial(_ffn_kernel, ns=ns, nt=nt, unroll=unroll),
        grid_spec=grid_spec,
        out_shape=jax.ShapeDtypeStruct((B, E, ns + SLOT_WIN, D), BF16),
        compiler_params=_cparams(3), name="expert_ffn",
    )(cnt, h2, pos.reshape(B, E, 1, Tt), w_gate, w_up, w_down)


def _combine_kernel(cnt_ref, pos_ref, aff_ref, y_ref, x_ref, mod_ref, o_ref, yw_ref, p_ref, acc_ref, *, D, ns, t0):
    b, t = pl.program_id(0), pl.program_id(1) + t0
    E = pos_ref.shape[1]
    W = SLOT_WIN
    starts, rounds = [], jnp.int32(0)
    for e in range(E):
        lo, hi = cnt_ref[b, e, t], cnt_ref[b, e, t + 1]
        s0 = (lo // 16) * 16
        starts.append(s0)
        rounds = jnp.maximum(rounds, jnp.where(hi > lo, (hi - s0 + W - 1) // W, 0))
    def round_sum(r):
        for e in range(E):
            base = pl.multiple_of(jnp.minimum(starts[e] + r * W, ns), 16)
            yw_ref[e * W:(e + 1) * W, :] = y_ref[0, e, pl.ds(base, W), :]
            srow = base + lax.broadcasted_iota(jnp.int32, (W, TM), 0)
            p_ref[e * W:(e + 1) * W, :] = jnp.where(pos_ref[0, e:e + 1, :] == srow, aff_ref[0, e:e + 1, :], 0.0).astype(BF16)
        return lax.dot_general(p_ref[...], yw_ref[...], TN_DIMS, preferred_element_type=F32)

    acc_ref[...] = round_sum(0)

    def extra_round(r, carry):
        acc_ref[...] += round_sum(r)
        return carry

    lax.fori_loop(1, rounds, extra_round, 0)
    o_ref[0] = x_ref[0] + mod_ref[0, 0][:, 5 * D:6 * D] * acc_ref[...]


def combine(cnt, pos, aff, Y, X1, modsel, latent_only):
    B, Tt, D = X1.shape
    E = pos.shape[1]
    nsp = Y.shape[2]
    t0 = 1 if latent_only else 0
    grid_spec = pltpu.PrefetchScalarGridSpec(
        num_scalar_prefetch=1,
        grid=(B, Tt // TM - t0),
        in_specs=[pl.BlockSpec((1, E, TM), lambda b, t, c: (b, 0, t + t0)),
                  pl.BlockSpec((1, E, TM), lambda b, t, c: (b, 0, t + t0)),
                  pl.BlockSpec((1, E, nsp, D), lambda b, t, c: (b, 0, 0, 0), pipeline_mode=pl.Buffered(1)),
                  pl.BlockSpec((1, TM, D), lambda b, t, c: (b, t + t0, 0)),
                  pl.BlockSpec((1, 1, 1, 6 * D), lambda b, t, c: (b, jnp.minimum(t + t0, 1), 0, 0))],
        out_specs=pl.BlockSpec((1, TM, D), lambda b, t, c: (b, t, 0)),
        scratch_shapes=[pltpu.VMEM((E * SLOT_WIN, D), BF16), pltpu.VMEM((E * SLOT_WIN, TM), BF16),
                        pltpu.VMEM((TM, D), F32)])
    return pl.pallas_call(
        functools.partial(_combine_kernel, D=D, ns=nsp - SLOT_WIN, t0=t0),
        grid_spec=grid_spec,
        out_shape=jax.ShapeDtypeStruct((B, Tt - t0 * TM, D), F32),
        compiler_params=_cparams(2), name="combine",
    )(cnt, pos, aff, Y, X1, modsel)


def _pack_w_in(w, D):
    widths = (NA_W, NA_W, NA_W, M_W, M_W, M_W, M_W, N_GATE_COLS, 2 * CONV_CH, 3 * D)
    offs = [0]
    for wd in widths:
        offs.append(offs[-1] + wd)
    seg = [w[:, offs[i]:offs[i + 1]] for i in range(len(widths))]
    return jnp.concatenate(seg[:7] + seg[8:], axis=1).astype(BF16), _gate_rows(seg[7].T).astype(BF16)


def _gate_rows(g):
    H = M_HEADS
    i_fw, f_fw, i_bw, f_bw = (g[j * H:(j + 1) * H] for j in range(4))
    return jnp.concatenate([i_fw, f_fw, f_fw, i_fw, i_bw, f_bw, f_bw, i_bw], axis=0)


def _rope_tables(T, L):
    nf = M_HEAD_DIM // 4
    t = jnp.arange(T)
    row = (t // GRID_W).astype(F32)
    col = (t % GRID_W).astype(F32)
    inv = ROPE_BASE ** (-jnp.arange(nf, dtype=F32) / nf)
    ar, ac = row[:, None] * inv[None, :], col[:, None] * inv[None, :]
    z = jnp.zeros((T, nf), F32)
    cos = jnp.concatenate([jnp.cos(ar), jnp.cos(ar), jnp.cos(ac), jnp.cos(ac)], axis=1)
    sina = jnp.concatenate([-jnp.sin(ar), z, -jnp.sin(ac), z], axis=1)
    sinb = jnp.concatenate([z, jnp.sin(ar), z, jnp.sin(ac)], axis=1)
    ident = lambda v: jnp.full((L, M_HEAD_DIM), v, F32)
    return (jnp.concatenate([ident(1.0), cos]), jnp.concatenate([ident(0.0), sina]),
            jnp.concatenate([ident(0.0), sinb]))


def kernel(x, c, ctx, c_ctx, w_mod, b_mod, w_in, na_q_gain, na_k_gain, na_rpb, m_gate_bias,
           m_norm_gain, conv_w, conv_b, conv_ln_g, conv_ln_b, w_br_na, w_br_m, w_br_conv, w_out,
           w_router, w_gate, w_up, w_down):
    B, T, D = x.shape
    L = ctx.shape[1]
    assert L == CTX_LEN == TM and T % TM == 0 and B + 1 <= 16
    depth = w_in.shape[0]
    X = jnp.concatenate([ctx, x], axis=1)
    cc = jnp.zeros((16, D), F32).at[:B].set(c).at[B].set(c_ctx)
    mods = modulation(cc, w_mod, b_mod)
    cos, sina, sinb = _rope_tables(T, L)
    hd = jnp.arange(NA_W) // NA_HEAD_DIM
    gmat = (hd[:, None] == hd[None, :]).astype(BF16)
    bias_tables = na_bias_tables(na_rpb * LOG2E)
    for l in range(depth):
        modsel = jnp.stack([jnp.broadcast_to(mods[l, B], (B, 6 * D)), mods[l, :B]], axis=1).reshape(B, 2, 1, 6 * D)
        q_gain = (jnp.tile(na_q_gain[l], NA_HEADS) * (NA_HEAD_DIM ** -0.5 * LOG2E)).reshape(1, NA_W)
        k_gain = jnp.tile(na_k_gain[l], NA_HEADS).reshape(1, NA_W)
        w_tok, w_gates_t = _pack_w_in(w_in[l], D)
        naq, nak, nav, mq, mk, mv, mo, mgf, mgb, cu, bg = input_projection(
            X, modsel, w_tok, q_gain, k_gain, gmat, cos, sina, sinb, w_gates_t,
            _gate_rows(m_gate_bias[l].reshape(N_GATE_COLS, 1)))
        na = na_attention(naq, nak, nav, bias_tables[l])
        hfw, hbw = mlstm(mq, mk, mv, mgf, mgb)
        X1, h2, aff = merge_and_route(
            na, hfw, hbw, mo, cu, bg, X, modsel, m_norm_gain[l].reshape(1, M_W),
            conv_w[l], conv_b[l], conv_ln_g[l], conv_ln_b[l],
            w_br_na[l].astype(BF16), w_br_m[l].astype(BF16), w_br_conv[l].astype(BF16), w_out[l].astype(BF16),
            w_router[l].T.astype(BF16))
        pos, cnt = expert_choice_slots(aff)
        cnt = cnt[:, :, :T // TM + 2]
        Y = expert_ffn(cnt, h2, pos, w_gate, w_up, w_down, l)
        X = combine(cnt, pos, aff, Y, X1, modsel, latent_only=(l == depth - 1))
    return X
```

```python
import functools

import jax
import jax.numpy as jnp
from jax import lax
from jax.experimental import pallas as pl
from jax.experimental.pallas import tpu as pltpu

DEPTH = 4
CTX_LEN = 256
GRID_W = 64
NA_HEADS = 8
NA_HEAD_DIM = 64
NA_KH = 8
NA_KW = 16
M_HEADS = 4
M_HEAD_DIM = 128
CONV_CH = 512
CONV_K = 31
N_EXPERTS = 16
EC_FACTOR = 2
ROPE_BASE = 10000.0
NORM_EPS = 1e-6

NA_W = NA_HEADS * NA_HEAD_DIM
M_W = M_HEADS * M_HEAD_DIM
N_GATE_COLS = 4 * M_HEADS

LANES = 128
TM = 256
M_CH = 128
CONV_HALO = 16
CONV_ROWS = 64
SLOT_WIN = 64
NEG = -1e30
LOG2E = 1.4426950408889634
VMEM_LIMIT_BYTES = 56 * 1024 * 1024

F32 = jnp.float32
BF16 = jnp.bfloat16
NT_DIMS = (((1,), (1,)), ((), ()))
TN_DIMS = (((0,), (0,)), ((), ()))


def _cparams(n_axes):
    return pltpu.CompilerParams(dimension_semantics=("arbitrary",) * n_axes,
                                vmem_limit_bytes=VMEM_LIMIT_BYTES)


def _sigmoid(x):
    return 1.0 / (1.0 + jnp.exp(-x))


def _modnorm(x, shift, scale):
    ms = jnp.mean(x * x, axis=-1, keepdims=True)
    return (x * lax.rsqrt(ms + NORM_EPS)) * (1.0 + scale) + shift


def _chunk(width):
    return 512 if width % 512 == 0 else 256


def _mod_kernel(c_ref, w_ref, b_ref, o_ref):
    c = c_ref[...]
    s = (c * _sigmoid(c)).astype(BF16)
    o_ref[0] = jnp.dot(s, w_ref[0].astype(BF16), preferred_element_type=F32) + b_ref[0]


def modulation(cc, w_mod, b_mod):
    depth, D, N = w_mod.shape
    tn = 1024 if N % 1024 == 0 else N
    return pl.pallas_call(
        _mod_kernel,
        grid=(depth, N // tn),
        in_specs=[pl.BlockSpec((16, D), lambda l, n: (0, 0)),
                  pl.BlockSpec((1, D, tn), lambda l, n: (l, 0, n)),
                  pl.BlockSpec((1, 1, tn), lambda l, n: (l, 0, n))],
        out_specs=pl.BlockSpec((1, 16, tn), lambda l, n: (l, 0, n)),
        out_shape=jax.ShapeDtypeStruct((depth, 16, N), F32),
        compiler_params=_cparams(2), name="modulation",
    )(cc, w_mod, b_mod.reshape(depth, 1, N))


def _inproj_kernel(x_ref, mod_ref, w_ref, qg_ref, kg_ref, gm_ref, cos_ref, sa_ref, sb_ref, wgt_ref, gb_ref,
                   naq_ref, nak_ref, nav_ref, mq_ref, mk_ref, mv_ref, mo_ref, mgf_ref, mgb_ref, cu_ref, bg_ref, *, D):
    mod = mod_ref[0, 0]
    h = _modnorm(x_ref[0], mod[:, 0:D], mod[:, D:2 * D]).astype(BF16)

    def proj(off, width):
        return jnp.dot(h, w_ref[:, off:off + width], preferred_element_type=F32)

    gm = gm_ref[...]
    for ref, gain_ref, off in ((naq_ref, qg_ref, 0), (nak_ref, kg_ref, NA_W)):
        y = proj(off, NA_W)
        ysq = y * y
        hi = ysq.astype(BF16)
        lo = (ysq - hi.astype(F32)).astype(BF16)
        ss = jnp.dot(hi, gm, preferred_element_type=F32) + jnp.dot(lo, gm, preferred_element_type=F32)
        ref[0] = (y * lax.rsqrt(ss * (1.0 / NA_HEAD_DIM) + NORM_EPS) * gain_ref[...]).astype(BF16)
    nav_ref[0] = proj(2 * NA_W, NA_W).astype(BF16)

    cos, sa, sb = cos_ref[...], sa_ref[...], sb_ref[...]
    q4 = M_HEAD_DIM // 4
    base = 3 * NA_W
    for ref, off, scale in ((mq_ref, base, 1.0), (mk_ref, base + M_W, M_HEAD_DIM ** -0.5)):
        y = proj(off, M_W)
        for hh in range(M_HEADS):
            yh = y[:, hh * M_HEAD_DIM:(hh + 1) * M_HEAD_DIM]
            r = yh * cos + pltpu.roll(yh, M_HEAD_DIM - q4, 1) * sa + pltpu.roll(yh, q4, 1) * sb
            ref[0, :, hh * M_HEAD_DIM:(hh + 1) * M_HEAD_DIM] = (r * scale).astype(BF16)
    mv_ref[0] = proj(base + 2 * M_W, M_W).astype(BF16)
    mo_ref[0] = proj(base + 3 * M_W, M_W).astype(BF16)
    base += 4 * M_W
    for ref, width in ((cu_ref, 2 * CONV_CH), (bg_ref, 3 * D)):
        cw = _chunk(width)
        for j in range(width // cw):
            ref[0, :, j * cw:(j + 1) * cw] = proj(base + j * cw, cw).astype(BF16)
        base += width
    g = lax.dot_general(wgt_ref[...], h, NT_DIMS, preferred_element_type=F32) + gb_ref[...]
    nrow = g.shape[0] // 2
    for cidx in range(g.shape[1] // M_CH):
        mgf_ref[0, cidx] = g[0:nrow, cidx * M_CH:(cidx + 1) * M_CH]
        mgb_ref[0, cidx] = g[nrow:2 * nrow, cidx * M_CH:(cidx + 1) * M_CH]


def input_projection(X, modsel, w_packed, q_gain, k_gain, gmat, cos, sina, sinb, w_gates_t, gate_bias):
    B, Tt, D = X.shape
    NP = w_packed.shape[1]
    nt = Tt // TM
    tok = lambda w: pl.BlockSpec((1, TM, w), lambda b, t: (b, t, 0))
    const = lambda shape: pl.BlockSpec(shape, lambda b, t: (0,) * len(shape))
    out_widths = (NA_W, NA_W, NA_W, M_W, M_W, M_W, M_W, None, None, 2 * CONV_CH, 3 * D)
    gate_rows = w_gates_t.shape[0] // 2
    gate_spec = pl.BlockSpec((1, TM // M_CH, gate_rows, M_CH), lambda b, t: (b, t, 0, 0))
    return pl.pallas_call(
        functools.partial(_inproj_kernel, D=D),
        grid=(B, nt),
        in_specs=[tok(D),
                  pl.BlockSpec((1, 1, 1, 6 * D), lambda b, t: (b, jnp.minimum(t, 1), 0, 0)),
                  pl.BlockSpec((D, NP), lambda b, t: (0, 0), pipeline_mode=pl.Buffered(1)),
                  const((1, NA_W)), const((1, NA_W)), const((NA_W, NA_W)),
                  pl.BlockSpec((TM, M_HEAD_DIM), lambda b, t: (t, 0)),
                  pl.BlockSpec((TM, M_HEAD_DIM), lambda b, t: (t, 0)),
                  pl.BlockSpec((TM, M_HEAD_DIM), lambda b, t: (t, 0)),
                  const((2 * gate_rows, D)), const((2 * gate_rows, 1))],
        out_specs=[gate_spec if w is None else tok(w) for w in out_widths],
        out_shape=[jax.ShapeDtypeStruct((B, Tt // M_CH, gate_rows, M_CH), F32) if w is None
                   else jax.ShapeDtypeStruct((B, Tt, w), BF16) for w in out_widths],
        compiler_params=_cparams(2), name="input_projection",
    )(X, modsel, w_packed, q_gain, k_gain, gmat, cos, sina, sinb, w_gates_t, gate_bias)


def _na_kernel(q_ref, k_ref, v_ref, bias_ref, o_ref, *, L, rows):
    t = pl.program_id(2)
    even = lax.broadcasted_iota(jnp.int32, (1, LANES), 1) < NA_HEAD_DIM
    pairs = [slice(jj * LANES, (jj + 1) * LANES) for jj in range(NA_PAIRS)]

    def heads_stack(q):
        z = jnp.zeros_like(q)
        return jnp.concatenate([jnp.where(even, q, z), jnp.where(even, z, q)], axis=0)

    def heads_merge(o):
        n = o.shape[0] // 2
        return jnp.where(even, o[:n], o[n:])

    @pl.when(t == 0)
    def _context_queries():
        for sl in pairs:
            kc, vc = k_ref[0, 0:L, sl], v_ref[0, 0:L, sl]
            q2 = heads_stack(q_ref[0, :, sl])
            s = lax.dot_general(q2, kc, NT_DIMS, preferred_element_type=F32)
            m = jnp.max(s, axis=-1, keepdims=True)
            p = jnp.exp2(s - m)
            l = jnp.sum(p, axis=-1, keepdims=True)
            o = jnp.dot(p.astype(BF16), vc, preferred_element_type=F32) / l
            o_ref[0, :, sl] = heads_merge(o).astype(BF16)

    @pl.when(t > 0)
    def _latent_queries():
        for rr in range(TM // GRID_W):
            r = (t - 1) * (TM // GRID_W) + rr
            rs = jnp.clip(r - NA_KH // 2, 0, rows - NA_KH)
            start = pl.multiple_of(L + rs * GRID_W, GRID_W)
            qrows = slice(rr * GRID_W, (rr + 1) * GRID_W)
            for jj, sl in enumerate(pairs):
                kc, vc = k_ref[0, 0:L, sl], v_ref[0, 0:L, sl]
                kw = k_ref[0, pl.ds(start, NA_KH * GRID_W), sl]
                vw = v_ref[0, pl.ds(start, NA_KH * GRID_W), sl]
                q2 = heads_stack(q_ref[0, qrows, sl])
                s_lat = lax.dot_general(q2, kw, NT_DIMS, preferred_element_type=F32) + bias_ref[jj, r - rs]
                s_ctx = lax.dot_general(q2, kc, NT_DIMS, preferred_element_type=F32)
                m = jnp.maximum(jnp.max(s_lat, axis=-1, keepdims=True), jnp.max(s_ctx, axis=-1, keepdims=True))
                p_lat = jnp.exp2(s_lat - m)
                p_ctx = jnp.exp2(s_ctx - m)
                l = jnp.sum(p_lat, axis=-1, keepdims=True) + jnp.sum(p_ctx, axis=-1, keepdims=True)
                o = (jnp.dot(p_lat.astype(BF16), vw, preferred_element_type=F32)
                     + jnp.dot(p_ctx.astype(BF16), vc, preferred_element_type=F32)) / l
                o_ref[0, qrows, sl] = heads_merge(o).astype(BF16)


NA_PAIRS = 4


def na_attention(q, k, v, bias):
    B, Tt, _ = q.shape
    L = CTX_LEN
    rows = (Tt - L) // GRID_W
    w = NA_PAIRS * LANES
    return pl.pallas_call(
        functools.partial(_na_kernel, L=L, rows=rows),
        grid=(B, NA_W // w, Tt // TM),
        in_specs=[pl.BlockSpec((1, TM, w), lambda b, j, t: (b, t, j)),
                  pl.BlockSpec((1, Tt, w), lambda b, j, t: (b, 0, j)),
                  pl.BlockSpec((1, Tt, w), lambda b, j, t: (b, 0, j)),
                  pl.BlockSpec((NA_PAIRS,) + bias.shape[1:], lambda b, j, t: (j, 0, 0, 0),
                               pipeline_mode=pl.Buffered(1))],
        out_specs=pl.BlockSpec((1, TM, w), lambda b, j, t: (b, t, j)),
        out_shape=jax.ShapeDtypeStruct((B, Tt, NA_W), BF16),
        compiler_params=_cparams(3), name="na_attention",
    )(q, k, v, bias)


def na_bias_tables(rpb):
    depth, H = rpb.shape[:2]
    cq = jnp.arange(GRID_W)
    col_start = jnp.clip(cq - NA_KW // 2, 0, GRID_W - NA_KW)
    col_mask = (cq[None, :] >= col_start[:, None]) & (cq[None, :] < col_start[:, None] + NA_KW)
    col_idx = jnp.clip(cq[None, :] - cq[:, None] + NA_KW - 1, 0, 2 * NA_KW - 2)
    pick = (col_idx[None] == jnp.arange(2 * NA_KW - 1)[:, None, None]).astype(F32)
    cols = jnp.einsum('lhrc,cqk->lhrqk', rpb, pick, precision=lax.Precision.HIGHEST)
    cols = jnp.where(col_mask, cols, NEG)
    tbl = jnp.stack([cols[:, :, NA_KH - 1 - dl:2 * NA_KH - 1 - dl] for dl in range(NA_KH)], axis=2)
    tbl = jnp.transpose(tbl, (0, 1, 2, 4, 3, 5)).reshape(depth, H // 2, 2, NA_KH, GRID_W, NA_KH * GRID_W)
    return jnp.transpose(tbl, (0, 1, 3, 2, 4, 5)).reshape(depth, H // 2, NA_KH, 2 * GRID_W, NA_KH * GRID_W)


def _log_sigmoid(x):
    return jnp.minimum(x, 0.0) - jnp.log(1.0 + jnp.exp(-jnp.abs(x)))


def _mlstm_gate_kernel(gf_ref, gb_ref, rqf_ref, rqb_ref, col_ref, t_scr):
    nc, n = gf_ref.shape[1], M_CH
    rows = nc * 8
    lane = lax.broadcasted_iota(jnp.int32, (rows, n), 1)
    for d, (g_ref, rq_ref) in enumerate(((gf_ref, rqf_ref), (gb_ref, rqb_ref))):
        def scan(x, op, fill):
            sh = 1
            while sh < n:
                if d == 0:
                    x = op(x, jnp.where(lane >= sh, pltpu.roll(x, sh, 1), fill))
                else:
                    x = op(x, jnp.where(lane < n - sh, pltpu.roll(x, n - sh, 1), fill))
                sh *= 2
            return x

        top = g_ref[0, :, 0:8, :].reshape(rows, n)
        bot = g_ref[0, :, 8:16, :].reshape(rows, n)
        b = scan(_log_sigmoid(bot), jnp.add, 0.0)
        last = n - 1 if d == 0 else 0
        tot = jnp.broadcast_to(b[:, last:last + 1], (rows, n))
        r = top - b
        cm = scan(r, jnp.maximum, NEG)
        amax = jnp.broadcast_to(jnp.max(tot + r, axis=1, keepdims=True), (rows, n))
        for j, val in enumerate((r, cm, tot, amax)):
            rq_ref[0, :, 8 * j:8 * (j + 1), :] = val.reshape(nc, 8, n)
        t_scr[:, 16 * d:16 * d + 8, :] = b.reshape(nc, 8, n)
        t_scr[:, 16 * d + 8:16 * d + 16, :] = cm.reshape(nc, 8, n)

    def transpose_chunk(ci, carry):
        tile = jnp.concatenate([t_scr[ci], jnp.zeros((n - 32, n), F32)], axis=0)
        col_ref[0, ci] = tile.T
        return carry

    lax.fori_loop(0, nc, transpose_chunk, 0)


def mlstm_gates(gf, gb):
    B, nc, _, n = gf.shape
    blk = lambda r: pl.BlockSpec((1, nc, r, n), lambda b: (b, 0, 0, 0))
    return pl.pallas_call(
        _mlstm_gate_kernel,
        grid=(B,),
        in_specs=[blk(16), blk(16)],
        out_specs=[blk(32), blk(32), blk(n)],
        out_shape=[jax.ShapeDtypeStruct((B, nc, 32, n), F32)] * 2 + [jax.ShapeDtypeStruct((B, nc, n, n), F32)],
        scratch_shapes=[pltpu.VMEM((nc, 32, n), F32)],
        compiler_params=_cparams(1), name="mlstm_gates",
    )(gf, gb)


def _mlstm_chain_group(d, q_ref, k_ref, v_ref, rq_ref, col_ref, o_ref, ct_ref, m_ref, bb):
    n, H = M_CH, M_HEADS
    row8 = lax.broadcasted_iota(jnp.int32, (2 * H, n), 0)
    lane = lax.broadcasted_iota(jnp.int32, (n, n), 1)
    sub = lax.broadcasted_iota(jnp.int32, (n, n), 0)
    eye = lane == sub
    causal = (lane <= sub) if d == 0 else (lane >= sub)
    one0 = jnp.where(lane == 0, 1.0, 0.0).astype(BF16)

    r8, cm8 = rq_ref[bb, 0, 0:8, :], rq_ref[bb, 0, 8:16, :]
    tot, amax = rq_ref[bb, 0, 16:24, :], rq_ref[bb, 0, 24:32, :]
    m0 = m_ref[d, bb]
    nm8 = jnp.maximum(m0, cm8)
    m_new = jnp.maximum(tot + m0, amax)
    decay8 = jnp.exp(tot + m0 - m_new)
    w8 = jnp.exp(tot + r8 - m_new)
    e8 = jnp.exp(m0 - nm8)
    m_ref[d, bb] = jnp.where(row8 < H, m_new, 0.0)
    tc = col_ref[bb, 0]

    for hh in range(H):
        sl = slice(hh * M_HEAD_DIM, (hh + 1) * M_HEAD_DIM)
        qh, kh, vh = q_ref[bb, :, sl], k_ref[bb, :, sl], v_ref[bb, :, sl]
        b_col = tc[:, 16 * d + hh:16 * d + hh + 1]
        nm_col = jnp.maximum(m0[hh:hh + 1, 0:1], tc[:, 16 * d + 8 + hh:16 * d + 8 + hh + 1])
        dmat = jnp.where(causal, jnp.exp(r8[hh:hh + 1, :] - nm_col), 0.0)
        s = (lax.dot_general(qh, kh, NT_DIMS, preferred_element_type=F32) * dmat).astype(BF16)
        e_diag = jnp.where(eye, e8[hh:hh + 1, :], 0.0).astype(BF16)
        w_diag = jnp.where(eye, w8[hh:hh + 1, :], 0.0).astype(BF16)
        vaug = jnp.concatenate([vh, one0], axis=1)
        ct = ct_ref[d, bb, hh]
        inter = jnp.dot(qh, ct.astype(BF16), preferred_element_type=F32).astype(BF16)
        num = jnp.dot(jnp.concatenate([s, e_diag], axis=1), jnp.concatenate([vaug, inter], axis=0),
                      preferred_element_type=F32)
        den = num[:, M_HEAD_DIM:M_HEAD_DIM + 1]
        o_ref[bb, :, sl] = num[:, :M_HEAD_DIM] / jnp.maximum(jnp.abs(den), jnp.exp(-(b_col + nm_col)))
        wv = jnp.dot(w_diag, vaug, preferred_element_type=F32).astype(BF16)
        ct_ref[d, bb, hh] = decay8[hh:hh + 1, 0:1] * ct + lax.dot_general(kh, wv, TN_DIMS, preferred_element_type=F32)


def _mlstm_kernel(qf_ref, kf_ref, vf_ref, rqf_ref, colf_ref, qb_ref, kb_ref, vb_ref, rqb_ref, colb_ref,
                  of_ref, ob_ref, ct_ref, m_ref):
    @pl.when(pl.program_id(1) == 0)
    def _():
        ct_ref[...] = jnp.zeros_like(ct_ref)
        m_ref[...] = jnp.zeros_like(m_ref)

    for bb in range(qf_ref.shape[0]):
        _mlstm_chain_group(0, qf_ref, kf_ref, vf_ref, rqf_ref, colf_ref, of_ref, ct_ref, m_ref, bb)
        _mlstm_chain_group(1, qb_ref, kb_ref, vb_ref, rqb_ref, colb_ref, ob_ref, ct_ref, m_ref, bb)


M_BATCH = 4


def mlstm(q, k, v, gf, gb):
    B, Tt, _ = q.shape
    nc = Tt // M_CH
    lc = CTX_LEN // M_CH
    nb = M_BATCH if B % M_BATCH == 0 else 1
    rqf, rqb, col = mlstm_gates(gf, gb)

    def bw(s):
        return jnp.where(s < lc, lc - 1 - s, nc - 1 + lc - s)

    tok_f = pl.BlockSpec((nb, M_CH, M_W), lambda b, s: (b, s, 0))
    tok_b = pl.BlockSpec((nb, M_CH, M_W), lambda b, s: (b, bw(s), 0))
    chunk_f = lambda r: pl.BlockSpec((nb, 1, r, M_CH), lambda b, s: (b, s, 0, 0))
    chunk_b = lambda r: pl.BlockSpec((nb, 1, r, M_CH), lambda b, s: (b, bw(s), 0, 0))
    return pl.pallas_call(
        _mlstm_kernel,
        grid=(B // nb, nc),
        in_specs=[tok_f, tok_f, tok_f, chunk_f(32), chunk_f(M_CH), tok_b, tok_b, tok_b, chunk_b(32), chunk_b(M_CH)],
        out_specs=[tok_f, tok_b],
        out_shape=[jax.ShapeDtypeStruct((B, Tt, M_W), F32)] * 2,
        scratch_shapes=[pltpu.VMEM((2, nb, M_HEADS, M_HEAD_DIM, M_HEAD_DIM + LANES), F32),
                        pltpu.VMEM((2, nb, 2 * M_HEADS, M_CH), F32)],
        compiler_params=_cparams(2), name="mlstm",
    )(q, k, v, rqf, col, q, k, v, rqb, col)


def _conv_tile(t, nt, cur_ref, prev_ref, next_ref, w_ref, b_ref, g_ref, be_ref, o_ref, hbuf, hsh):
    C = CONV_CH

    def glu(u):
        u = u.astype(F32)
        return u[:, :C] * _sigmoid(u[:, C:])

    prev_ok = t >= 2
    next_ok = jnp.logical_and(t >= 1, t < nt - 1)
    hbuf[0:CONV_HALO] = jnp.where(prev_ok, glu(prev_ref[0]), 0.0)
    hbuf[CONV_HALO:CONV_HALO + TM] = glu(cur_ref[0])
    hbuf[CONV_HALO + TM:2 * CONV_HALO + TM] = jnp.where(next_ok, glu(next_ref[0]), 0.0)
    span = TM + 2 * CONV_HALO - 8
    for r in range(8):
        hsh[r] = hbuf[r:r + span, :]
    off = CONV_HALO - CONV_K // 2
    for rc in range(TM // CONV_ROWS):
        acc = jnp.zeros((CONV_ROWS, C), F32) + b_ref[...]
        for j in range(CONV_K):
            a, r = divmod(off + j, 8)
            acc = acc + hsh[r, rc * CONV_ROWS + 8 * a:rc * CONV_ROWS + 8 * a + CONV_ROWS, :] * w_ref[j:j + 1, :]
        mu = jnp.mean(acc, axis=-1, keepdims=True)
        xc = acc - mu
        var = jnp.mean(xc * xc, axis=-1, keepdims=True)
        y = xc * lax.rsqrt(var + NORM_EPS) * g_ref[...] + be_ref[...]
        o_ref[rc * CONV_ROWS:(rc + 1) * CONV_ROWS, :] = (y * _sigmoid(y)).astype(BF16)


def _merge_kernel(na_ref, hfw_ref, hbw_ref, mo_ref, cu_ref, cup_ref, cun_ref, bg_ref, x_ref, mod_ref, ng_ref,
                  cw_ref, cb_ref, cg_ref, cbe_ref, wna_ref, wm_ref, wc_ref, wo_ref, wr_ref,
                  x1_ref, h2_ref, aff_ref, cv_ref, hbuf, hsh, *, D, nt):
    _conv_tile(pl.program_id(1), nt, cu_ref, cup_ref, cun_ref, cw_ref, cb_ref, cg_ref, cbe_ref, cv_ref, hbuf, hsh)
    mod = mod_ref[0, 0]
    mh = hfw_ref[0] + hbw_ref[0]
    parts = []
    for hh in range(M_HEADS):
        xh = mh[:, hh * M_HEAD_DIM:(hh + 1) * M_HEAD_DIM]
        parts.append(xh * lax.rsqrt(jnp.mean(xh * xh, axis=-1, keepdims=True) + NORM_EPS))
    mn = (jnp.concatenate(parts, axis=1) * ng_ref[...] * _sigmoid(mo_ref[0].astype(F32))).astype(BF16)
    z = (_sigmoid(bg_ref[0, :, 0:D].astype(F32)) * jnp.dot(na_ref[0], wna_ref[...], preferred_element_type=F32)
         + _sigmoid(bg_ref[0, :, D:2 * D].astype(F32)) * jnp.dot(mn, wm_ref[...], preferred_element_type=F32)
         + _sigmoid(bg_ref[0, :, 2 * D:3 * D].astype(F32)) * jnp.dot(cv_ref[...], wc_ref[...], preferred_element_type=F32))
    y = jnp.dot(z.astype(BF16), wo_ref[...], preferred_element_type=F32)
    x1 = x_ref[0] + mod[:, 2 * D:3 * D] * y
    x1_ref[0] = x1
    h2 = _modnorm(x1, mod[:, 3 * D:4 * D], mod[:, 4 * D:5 * D]).astype(BF16)
    h2_ref[0] = h2
    logits = lax.dot_general(wr_ref[...], h2, NT_DIMS, preferred_element_type=F32)
    e = jnp.exp(logits - jnp.max(logits, axis=0, keepdims=True))
    aff_ref[0] = e / jnp.sum(e, axis=0, keepdims=True)


def merge_and_route(na, hfw, hbw, mo, cu, bg, X, modsel, norm_gain, conv_w, conv_b, ln_g, ln_b,
                    w_na, w_m, w_c, w_o, w_rt):
    B, Tt, D = X.shape
    E = w_rt.shape[0]
    C = CONV_CH
    nt = Tt // TM
    hb = TM // CONV_HALO
    nhb = Tt // CONV_HALO
    wpad = jnp.zeros((32, C), F32).at[:CONV_K].set(conv_w)
    vec = lambda a: a.reshape(1, C)
    tok = lambda w: pl.BlockSpec((1, TM, w), lambda b, t: (b, t, 0))
    const = lambda shape: pl.BlockSpec(shape, lambda b, t: (0,) * len(shape))
    return pl.pallas_call(
        functools.partial(_merge_kernel, D=D, nt=nt),
        grid=(B, nt),
        in_specs=[tok(NA_W), tok(M_W), tok(M_W), tok(M_W), tok(2 * C),
                  pl.BlockSpec((1, CONV_HALO, 2 * C), lambda b, t: (b, jnp.maximum(t * hb - 1, 0), 0)),
                  pl.BlockSpec((1, CONV_HALO, 2 * C), lambda b, t: (b, jnp.minimum((t + 1) * hb, nhb - 1), 0)),
                  tok(3 * D), tok(D),
                  pl.BlockSpec((1, 1, 1, 6 * D), lambda b, t: (b, jnp.minimum(t, 1), 0, 0)),
                  const((1, M_W)), const((32, C)), const((1, C)), const((1, C)), const((1, C)),
                  const((NA_W, D)), const((M_W, D)), const((C, D)), const((D, D)), const((E, D))],
        out_specs=[tok(D), tok(D), pl.BlockSpec((1, E, TM), lambda b, t: (b, 0, t))],
        out_shape=[jax.ShapeDtypeStruct((B, Tt, D), F32), jax.ShapeDtypeStruct((B, Tt, D), BF16),
                   jax.ShapeDtypeStruct((B, E, Tt), F32)],
        scratch_shapes=[pltpu.VMEM((TM, C), BF16), pltpu.VMEM((TM + 2 * CONV_HALO, C), F32),
                        pltpu.VMEM((8, TM + 2 * CONV_HALO - 8, C), F32)],
        compiler_params=_cparams(2), name="merge_and_route",
    )(na, hfw, hbw, mo, cu, cu, cu, bg, X, modsel, norm_gain, wpad, vec(conv_b), vec(ln_g), vec(ln_b),
      w_na, w_m, w_c, w_o, w_rt)


def _topk_kernel(aff_ref, pos_ref, cnt_ref, *, L, k_ctx, k_lat):
    E, Tt = aff_ref.shape[1:]
    u = pltpu.bitcast(aff_ref[0], jnp.int32)
    is_ctx = lax.broadcasted_iota(jnp.int32, (E, Tt), 1) < L

    def count(mask):
        f = jnp.where(mask, 1.0, 0.0)
        c_ctx = jnp.sum(jnp.where(is_ctx, f, 0.0), axis=1, keepdims=True)
        return c_ctx, jnp.sum(f, axis=1, keepdims=True) - c_ctx

    def bit_step(i, carry):
        v_ctx, v_lat = carry
        bit = jnp.left_shift(jnp.int32(1), 30 - i)
        c_ctx, c_lat = count(u >= jnp.where(is_ctx, v_ctx | bit, v_lat | bit))
        return (jnp.where(c_ctx >= k_ctx, v_ctx | bit, v_ctx), jnp.where(c_lat >= k_lat, v_lat | bit, v_lat))

    zero = jnp.zeros((E, 1), jnp.int32)
    v_ctx, v_lat = lax.fori_loop(0, 31, bit_step, (zero, zero))
    thr = jnp.where(is_ctx, v_ctx, v_lat)
    g_ctx, g_lat = count(u > thr)
    need_ctx, need_lat = k_ctx - g_ctx, k_lat - g_lat

    r = lax.broadcasted_iota(jnp.int32, (TM, TM), 0)
    c = lax.broadcasted_iota(jnp.int32, (TM, TM), 1)
    tri = jnp.where(r < c, 1.0, 0.0).astype(BF16)
    lane_t = lax.broadcasted_iota(jnp.int32, (E, LANES), 1)
    cnt = jnp.zeros((E, LANES), F32)
    ties = jnp.zeros((E, 1), F32)
    slots = jnp.zeros((E, 1), F32)
    nt = Tt // TM
    for t in range(nt):
        if t == 1:
            ties = jnp.zeros((E, 1), F32)
        need = need_ctx if t == 0 else need_lat
        ut = pltpu.bitcast(aff_ref[0, :, t * TM:(t + 1) * TM], jnp.int32)
        tt = v_ctx if t == 0 else v_lat
        eq = jnp.where(ut == tt, 1.0, 0.0)
        rank = ties + jnp.dot(eq.astype(BF16), tri, preferred_element_type=F32)
        sel = jnp.where(ut > tt, 1.0, jnp.where(rank < need, eq, 0.0))
        pos = slots + jnp.dot(sel.astype(BF16), tri, preferred_element_type=F32)
        pos_ref[0, :, t * TM:(t + 1) * TM] = jnp.where(sel > 0.0, pos, -1.0).astype(jnp.int32)
        cnt = jnp.where(lane_t == t, slots, cnt)
        ties = ties + jnp.sum(eq, axis=1, keepdims=True)
        slots = slots + jnp.sum(sel, axis=1, keepdims=True)
    cnt = jnp.where(lane_t == nt, slots, cnt)
    cnt_ref[0] = cnt.astype(jnp.int32)


def expert_choice_slots(aff):
    B, E, Tt = aff.shape
    L = CTX_LEN
    k_ctx = EC_FACTOR * L // N_EXPERTS
    k_lat = EC_FACTOR * (Tt - L) // N_EXPERTS
    assert Tt // TM + 1 <= LANES
    return pl.pallas_call(
        functools.partial(_topk_kernel, L=L, k_ctx=k_ctx, k_lat=k_lat),
        grid=(B,),
        in_specs=[pl.BlockSpec((1, E, Tt), lambda b: (b, 0, 0))],
        out_specs=[pl.BlockSpec((1, E, Tt), lambda b: (b, 0, 0)), pl.BlockSpec((1, E, LANES), lambda b: (b, 0, 0))],
        out_shape=[jax.ShapeDtypeStruct((B, E, Tt), jnp.int32), jax.ShapeDtypeStruct((B, E, LANES), jnp.int32)],
        compiler_params=_cparams(1), name="expert_choice_slots",
    )(aff)


def _ffn_kernel(cnt_ref, h_ref, pos_ref, wg_ref, wu_ref, wd_ref, y_ref, xin_ref, acc_ref, *, ns, nt, unroll):
    b, e, f = pl.program_id(0), pl.program_id(1), pl.program_id(2)
    W = SLOT_WIN
    G = xin_ref.shape[0]
    eg = lax.rem(e, G)

    @pl.when(f == 0)
    def _():
        acc_ref[...] = jnp.zeros_like(acc_ref)

    @pl.when(jnp.logical_and(f == 0, eg == 0))
    def _gather():
        xin_ref[...] = jnp.zeros_like(xin_ref)

        def onehot(q, t, base):
            srow = base + lax.broadcasted_iota(jnp.int32, (W, TM), 0)
            return jnp.where(pos_ref[0, q, :, pl.ds(pl.multiple_of(t * TM, TM), TM)] == srow, 1.0, 0.0).astype(BF16)

        def rows_of(t, onehots):
            return jnp.dot(onehots, h_ref[0, pl.ds(pl.multiple_of(t * TM, TM), TM), :], preferred_element_type=F32)

        def add_rows(q, base, rows):
            xin_ref[q, pl.ds(base, W), :] = xin_ref[q, pl.ds(base, W), :] + rows.astype(BF16)

        def group_body(g, carry):
            tiles = [g * unroll + k for k in range(unroll)]
            bases = [[pl.multiple_of((cnt_ref[b, e + q, t] // 16) * 16, 16) for q in range(G)] for t in tiles]
            first = [rows_of(t, jnp.concatenate([onehot(q, t, bs[q]) for q in range(G)], axis=0))
                     for t, bs in zip(tiles, bases)]
            for t, bs, rows in zip(tiles, bases, first):
                for q in range(G):
                    add_rows(q, bs[q], rows[q * W:(q + 1) * W])

                    def more(r, c2, t=t, q=q, base=bs[q]):
                        nxt = pl.multiple_of(base + r * W, 16)
                        add_rows(q, nxt, rows_of(t, onehot(q, t, nxt)))
                        return c2

                    lax.fori_loop(1, (cnt_ref[b, e + q, t + 1] - bs[q] + W - 1) // W, more, 0)
            return carry

        lax.fori_loop(0, nt // unroll, group_body, 0)

    x = xin_ref[eg, 0:ns, :]
    a = jnp.dot(x, wg_ref[0, 0].astype(BF16), preferred_element_type=F32)
    u = jnp.dot(x, wu_ref[0, 0].astype(BF16), preferred_element_type=F32)
    hmid = (a * _sigmoid(a) * u).astype(BF16)
    acc_ref[...] += jnp.dot(hmid, wd_ref[0, 0].astype(BF16), preferred_element_type=F32)

    @pl.when(f == pl.num_programs(2) - 1)
    def _():
        y_ref[0, 0, 0:ns, :] = acc_ref[...].astype(BF16)
        y_ref[0, 0, ns:ns + SLOT_WIN, :] = jnp.zeros((SLOT_WIN, y_ref.shape[-1]), BF16)


FFN_GATHER_EXPERTS = 4


def expert_ffn(cnt, h2, pos, w_gate, w_up, w_down, layer):
    B, Tt, D = h2.shape
    _, E, _, F = w_gate.shape
    L = CTX_LEN
    ns = EC_FACTOR * L // N_EXPERTS + EC_FACTOR * (Tt - L) // N_EXPERTS
    assert ns % 16 == 0
    tf = min(512, F)
    nt = Tt // TM
    unroll = max(u for u in range(1, 4) if nt % u == 0)
    G = FFN_GATHER_EXPERTS
    assert E % G == 0
    grid_spec = pltpu.PrefetchScalarGridSpec(
        num_scalar_prefetch=1,
        grid=(B, E, F // tf),
        in_specs=[pl.BlockSpec((1, Tt, D), lambda b, e, f, c: (b, 0, 0), pipeline_mode=pl.Buffered(1)),
                  pl.BlockSpec((1, G, 1, Tt), lambda b, e, f, c: (b, e // G, 0, 0)),
                  pl.BlockSpec((1, 1, D, tf), lambda b, e, f, c: (layer, e, 0, f)),
                  pl.BlockSpec((1, 1, D, tf), lambda b, e, f, c: (layer, e, 0, f)),
                  pl.BlockSpec((1, 1, tf, D), lambda b, e, f, c: (layer, e, f, 0))],
        out_specs=pl.BlockSpec((1, 1, ns + SLOT_WIN, D), lambda b, e, f, c: (b, e, 0, 0)),
        scratch_shapes=[pltpu.VMEM((G, ns + SLOT_WIN, D), BF16), pltpu.VMEM((ns, D), F32)])
    return pl.pallas_call(
        functools.partial(_ffn_kernel, ns=ns, nt=nt, unroll=unroll),
        grid_spec=grid_spec,
        out_shape=jax.ShapeDtypeStruct((B, E, ns + SLOT_WIN, D), BF16),
        compiler_params=_cparams(3), name="expert_ffn",
    )(cnt, h2, pos.reshape(B, E, 1, Tt), w_gate, w_up, w_down)


def _combine_kernel(cnt_ref, pos_ref, aff_ref, y_ref, x_ref, mod_ref, o_ref, yw_ref, p_ref, acc_ref, *, D, ns, t0):
    b, t = pl.program_id(0), pl.program_id(1) + t0
    E = pos_ref.shape[1]
    W = SLOT_WIN
    starts, rounds = [], jnp.int32(0)
    for e in range(E):
        lo, hi = cnt_ref[b, e, t], cnt_ref[b, e, t + 1]
        s0 = (lo // 16) * 16
        starts.append(s0)
        rounds = jnp.maximum(rounds, jnp.where(hi > lo, (hi - s0 + W - 1) // W, 0))
    def round_sum(r):
        for e in range(E):
            base = pl.multiple_of(jnp.minimum(starts[e] + r * W, ns), 16)
            yw_ref[e * W:(e + 1) * W, :] = y_ref[0, e, pl.ds(base, W), :]
            srow = base + lax.broadcasted_iota(jnp.int32, (W, TM), 0)
            p_ref[e * W:(e + 1) * W, :] = jnp.where(pos_ref[0, e:e + 1, :] == srow, aff_ref[0, e:e + 1, :], 0.0).astype(BF16)
        return lax.dot_general(p_ref[...], yw_ref[...], TN_DIMS, preferred_element_type=F32)

    acc_ref[...] = round_sum(0)

    def extra_round(r, carry):
        acc_ref[...] += round_sum(r)
        return carry

    lax.fori_loop(1, rounds, extra_round, 0)
    o_ref[0] = x_ref[0] + mod_ref[0, 0][:, 5 * D:6 * D] * acc_ref[...]


def combine(cnt, pos, aff, Y, X1, modsel, latent_only):
    B, Tt, D = X1.shape
    E = pos.shape[1]
    nsp = Y.shape[2]
    t0 = 1 if latent_only else 0
    grid_spec = pltpu.PrefetchScalarGridSpec(
        num_scalar_prefetch=1,
        grid=(B, Tt // TM - t0),
        in_specs=[pl.BlockSpec((1, E, TM), lambda b, t, c: (b, 0, t + t0)),
                  pl.BlockSpec((1, E, TM), lambda b, t, c: (b, 0, t + t0)),
                  pl.BlockSpec((1, E, nsp, D), lambda b, t, c: (b, 0, 0, 0), pipeline_mode=pl.Buffered(1)),
                  pl.BlockSpec((1, TM, D), lambda b, t, c: (b, t + t0, 0)),
                  pl.BlockSpec((1, 1, 1, 6 * D), lambda b, t, c: (b, jnp.minimum(t + t0, 1), 0, 0))],
        out_specs=pl.BlockSpec((1, TM, D), lambda b, t, c: (b, t, 0)),
        scratch_shapes=[pltpu.VMEM((E * SLOT_WIN, D), BF16), pltpu.VMEM((E * SLOT_WIN, TM), BF16),
                        pltpu.VMEM((TM, D), F32)])
    return pl.pallas_call(
        functools.partial(_combine_kernel, D=D, ns=nsp - SLOT_WIN, t0=t0),
        grid_spec=grid_spec,
        out_shape=jax.ShapeDtypeStruct((B, Tt - t0 * TM, D), F32),
        compiler_params=_cparams(2), name="combine",
    )(cnt, pos, aff, Y, X1, modsel)


def _pack_w_in(w, D):
    widths = (NA_W, NA_W, NA_W, M_W, M_W, M_W, M_W, N_GATE_COLS, 2 * CONV_CH, 3 * D)
    offs = [0]
    for wd in widths:
        offs.append(offs[-1] + wd)
    seg = [w[:, offs[i]:offs[i + 1]] for i in range(len(widths))]
    return jnp.concatenate(seg[:7] + seg[8:], axis=1).astype(BF16), _gate_rows(seg[7].T).astype(BF16)


def _gate_rows(g):
    H = M_HEADS
    i_fw, f_fw, i_bw, f_bw = (g[j * H:(j + 1) * H] for j in range(4))
    return jnp.concatenate([i_fw, f_fw, f_fw, i_fw, i_bw, f_bw, f_bw, i_bw], axis=0)


def _rope_tables(T, L):
    nf = M_HEAD_DIM // 4
    t = jnp.arange(T)
    row = (t // GRID_W).astype(F32)
    col = (t % GRID_W).astype(F32)
    inv = ROPE_BASE ** (-jnp.arange(nf, dtype=F32) / nf)
    ar, ac = row[:, None] * inv[None, :], col[:, None] * inv[None, :]
    z = jnp.zeros((T, nf), F32)
    cos = jnp.concatenate([jnp.cos(ar), jnp.cos(ar), jnp.cos(ac), jnp.cos(ac)], axis=1)
    sina = jnp.concatenate([-jnp.sin(ar), z, -jnp.sin(ac), z], axis=1)
    sinb = jnp.concatenate([z, jnp.sin(ar), z, jnp.sin(ac)], axis=1)
    ident = lambda v: jnp.full((L, M_HEAD_DIM), v, F32)
    return (jnp.concatenate([ident(1.0), cos]), jnp.concatenate([ident(0.0), sina]),
            jnp.concatenate([ident(0.0), sinb]))


def kernel(x, c, ctx, c_ctx, w_mod, b_mod, w_in, na_q_gain, na_k_gain, na_rpb, m_gate_bias,
           m_norm_gain, conv_w, conv_b, conv_ln_g, conv_ln_b, w_br_na, w_br_m, w_br_conv, w_out,
           w_router, w_gate, w_up, w_down):
    B, T, D = x.shape
    L = ctx.shape[1]
    assert L == CTX_LEN == TM and T % TM == 0 and B + 1 <= 16
    depth = w_in.shape[0]
    X = jnp.concatenate([ctx, x], axis=1)
    cc = jnp.zeros((16, D), F32).at[:B].set(c).at[B].set(c_ctx)
    mods = modulation(cc, w_mod, b_mod)
    cos, sina, sinb = _rope_tables(T, L)
    hd = jnp.arange(NA_W) // NA_HEAD_DIM
    gmat = (hd[:, None] == hd[None, :]).astype(BF16)
    bias_tables = na_bias_tables(na_rpb * LOG2E)
    for l in range(depth):
        modsel = jnp.stack([jnp.broadcast_to(mods[l, B], (B, 6 * D)), mods[l, :B]], axis=1).reshape(B, 2, 1, 6 * D)
        q_gain = (jnp.tile(na_q_gain[l], NA_HEADS) * (NA_HEAD_DIM ** -0.5 * LOG2E)).reshape(1, NA_W)
        k_gain = jnp.tile(na_k_gain[l], NA_HEADS).reshape(1, NA_W)
        w_tok, w_gates_t = _pack_w_in(w_in[l], D)
        naq, nak, nav, mq, mk, mv, mo, mgf, mgb, cu, bg = input_projection(
            X, modsel, w_tok, q_gain, k_gain, gmat, cos, sina, sinb, w_gates_t,
            _gate_rows(m_gate_bias[l].reshape(N_GATE_COLS, 1)))
        na = na_attention(naq, nak, nav, bias_tables[l])
        hfw, hbw = mlstm(mq, mk, mv, mgf, mgb)
        X1, h2, aff = merge_and_route(
            na, hfw, hbw, mo, cu, bg, X, modsel, m_norm_gain[l].reshape(1, M_W),
            conv_w[l], conv_b[l], conv_ln_g[l], conv_ln_b[l],
            w_br_na[l].astype(BF16), w_br_m[l].astype(BF16), w_br_conv[l].astype(BF16), w_out[l].astype(BF16),
            w_router[l].T.astype(BF16))
        pos, cnt = expert_choice_slots(aff)
        cnt = cnt[:, :, :T // TM + 2]
        Y = expert_ffn(cnt, h2, pos, w_gate, w_up, w_down, l)
        X = combine(cnt, pos, aff, Y, X1, modsel, latent_only=(l == depth - 1))
    return X
```

```python
import functools

import jax
import jax.numpy as jnp
from jax import lax
from jax.experimental import pallas as pl
from jax.experimental.pallas import tpu as pltpu

DEPTH = 4
CTX_LEN = 256
GRID_W = 64
NA_HEADS = 8
NA_HEAD_DIM = 64
NA_KH = 8
NA_KW = 16
M_HEADS = 4
M_HEAD_DIM = 128
CONV_CH = 512
CONV_K = 31
N_EXPERTS = 16
EC_FACTOR = 2
ROPE_BASE = 10000.0
NORM_EPS = 1e-6

NA_W = NA_HEADS * NA_HEAD_DIM
M_W = M_HEADS * M_HEAD_DIM
N_GATE_COLS = 4 * M_HEADS

LANES = 128
TM = 256
M_CH = 128
CONV_HALO = 16
CONV_ROWS = 64
SLOT_WIN = 64
NEG = -1e30
LOG2E = 1.4426950408889634
VMEM_LIMIT_BYTES = 56 * 1024 * 1024

F32 = jnp.float32
BF16 = jnp.bfloat16
NT_DIMS = (((1,), (1,)), ((), ()))
TN_DIMS = (((0,), (0,)), ((), ()))


def _cparams(n_axes):
    return pltpu.CompilerParams(dimension_semantics=("arbitrary",) * n_axes,
                                vmem_limit_bytes=VMEM_LIMIT_BYTES)


def _sigmoid(x):
    return 1.0 / (1.0 + jnp.exp(-x))


def _modnorm(x, shift, scale):
    ms = jnp.mean(x * x, axis=-1, keepdims=True)
    return (x * lax.rsqrt(ms + NORM_EPS)) * (1.0 + scale) + shift


def _chunk(width):
    return 512 if width % 512 == 0 else 256


def _mod_kernel(c_ref, w_ref, b_ref, o_ref):
    c = c_ref[...]
    s = (c * _sigmoid(c)).astype(BF16)
    o_ref[0] = jnp.dot(s, w_ref[0].astype(BF16), preferred_element_type=F32) + b_ref[0]


def modulation(cc, w_mod, b_mod):
    depth, D, N = w_mod.shape
    tn = 1024 if N % 1024 == 0 else N
    return pl.pallas_call(
        _mod_kernel,
        grid=(depth, N // tn),
        in_specs=[pl.BlockSpec((16, D), lambda l, n: (0, 0)),
                  pl.BlockSpec((1, D, tn), lambda l, n: (l, 0, n)),
                  pl.BlockSpec((1, 1, tn), lambda l, n: (l, 0, n))],
        out_specs=pl.BlockSpec((1, 16, tn), lambda l, n: (l, 0, n)),
        out_shape=jax.ShapeDtypeStruct((depth, 16, N), F32),
        compiler_params=_cparams(2), name="modulation",
    )(cc, w_mod, b_mod.reshape(depth, 1, N))


def _inproj_kernel(x_ref, mod_ref, w_ref, qg_ref, kg_ref, gm_ref, cos_ref, sa_ref, sb_ref, wgt_ref, gb_ref,
                   naq_ref, nak_ref, nav_ref, mq_ref, mk_ref, mv_ref, mo_ref, mgf_ref, mgb_ref, cu_ref, bg_ref, *, D):
    mod = mod_ref[0, 0]
    h = _modnorm(x_ref[0], mod[:, 0:D], mod[:, D:2 * D]).astype(BF16)

    def proj(off, width):
        return jnp.dot(h, w_ref[:, off:off + width], preferred_element_type=F32)

    gm = gm_ref[...]
    for ref, gain_ref, off in ((naq_ref, qg_ref, 0), (nak_ref, kg_ref, NA_W)):
        y = proj(off, NA_W)
        ysq = y * y
        hi = ysq.astype(BF16)
        lo = (ysq - hi.astype(F32)).astype(BF16)
        ss = jnp.dot(hi, gm, preferred_element_type=F32) + jnp.dot(lo, gm, preferred_element_type=F32)
        ref[0] = (y * lax.rsqrt(ss * (1.0 / NA_HEAD_DIM) + NORM_EPS) * gain_ref[...]).astype(BF16)
    nav_ref[0] = proj(2 * NA_W, NA_W).astype(BF16)

    cos, sa, sb = cos_ref[...], sa_ref[...], sb_ref[...]
    q4 = M_HEAD_DIM // 4
    base = 3 * NA_W
    for ref, off, scale in ((mq_ref, base, 1.0), (mk_ref, base + M_W, M_HEAD_DIM ** -0.5)):
        y = proj(off, M_W)
        for hh in range(M_HEADS):
            yh = y[:, hh * M_HEAD_DIM:(hh + 1) * M_HEAD_DIM]
            r = yh * cos + pltpu.roll(yh, M_HEAD_DIM - q4, 1) * sa + pltpu.roll(yh, q4, 1) * sb
            ref[0, :, hh * M_HEAD_DIM:(hh + 1) * M_HEAD_DIM] = (r * scale).astype(BF16)
    mv_ref[0] = proj(base + 2 * M_W, M_W).astype(BF16)
    mo_ref[0] = proj(base + 3 * M_W, M_W).astype(BF16)
    base += 4 * M_W
    for ref, width in ((cu_ref, 2 * CONV_CH), (bg_ref, 3 * D)):
        cw = _chunk(width)
        for j in range(width // cw):
            ref[0, :, j * cw:(j + 1) * cw] = proj(base + j * cw, cw).astype(BF16)
        base += width
    g = lax.dot_general(wgt_ref[...], h, NT_DIMS, preferred_element_type=F32) + gb_ref[...]
    nrow = g.shape[0] // 2
    for cidx in range(g.shape[1] // M_CH):
        mgf_ref[0, cidx] = g[0:nrow, cidx * M_CH:(cidx + 1) * M_CH]
        mgb_ref[0, cidx] = g[nrow:2 * nrow, cidx * M_CH:(cidx + 1) * M_CH]


def input_projection(X, modsel, w_packed, q_gain, k_gain, gmat, cos, sina, sinb, w_gates_t, gate_bias):
    B, Tt, D = X.shape
    NP = w_packed.shape[1]
    nt = Tt // TM
    tok = lambda w: pl.BlockSpec((1, TM, w), lambda b, t: (b, t, 0))
    const = lambda shape: pl.BlockSpec(shape, lambda b, t: (0,) * len(shape))
    out_widths = (NA_W, NA_W, NA_W, M_W, M_W, M_W, M_W, None, None, 2 * CONV_CH, 3 * D)
    gate_rows = w_gates_t.shape[0] // 2
    gate_spec = pl.BlockSpec((1, TM // M_CH, gate_rows, M_CH), lambda b, t: (b, t, 0, 0))
    return pl.pallas_call(
        functools.partial(_inproj_kernel, D=D),
        grid=(B, nt),
        in_specs=[tok(D),
                  pl.BlockSpec((1, 1, 1, 6 * D), lambda b, t: (b, jnp.minimum(t, 1), 0, 0)),
                  pl.BlockSpec((D, NP), lambda b, t: (0, 0), pipeline_mode=pl.Buffered(1)),
                  const((1, NA_W)), const((1, NA_W)), const((NA_W, NA_W)),
                  pl.BlockSpec((TM, M_HEAD_DIM), lambda b, t: (t, 0)),
                  pl.BlockSpec((TM, M_HEAD_DIM), lambda b, t: (t, 0)),
                  pl.BlockSpec((TM, M_HEAD_DIM), lambda b, t: (t, 0)),
                  const((2 * gate_rows, D)), const((2 * gate_rows, 1))],
        out_specs=[gate_spec if w is None else tok(w) for w in out_widths],
        out_shape=[jax.ShapeDtypeStruct((B, Tt // M_CH, gate_rows, M_CH), F32) if w is None
                   else jax.ShapeDtypeStruct((B, Tt, w), BF16) for w in out_widths],
        compiler_params=_cparams(2), name="input_projection",
    )(X, modsel, w_packed, q_gain, k_gain, gmat, cos, sina, sinb, w_gates_t, gate_bias)


def _na_kernel(q_ref, k_ref, v_ref, bias_ref, o_ref, *, L, rows):
    t = pl.program_id(2)
    even = lax.broadcasted_iota(jnp.int32, (1, LANES), 1) < NA_HEAD_DIM
    pairs = [slice(jj * LANES, (jj + 1) * LANES) for jj in range(NA_PAIRS)]

    def heads_stack(q):
        z = jnp.zeros_like(q)
        return jnp.concatenate([jnp.where(even, q, z), jnp.where(even, z, q)], axis=0)

    def heads_merge(o):
        n = o.shape[0] // 2
        return jnp.where(even, o[:n], o[n:])

    @pl.when(t == 0)
    def _context_queries():
        for sl in pairs:
            kc, vc = k_ref[0, 0:L, sl], v_ref[0, 0:L, sl]
            q2 = heads_stack(q_ref[0, :, sl])
            s = lax.dot_general(q2, kc, NT_DIMS, preferred_element_type=F32)
            m = jnp.max(s, axis=-1, keepdims=True)
            p = jnp.exp2(s - m)
            l = jnp.sum(p, axis=-1, keepdims=True)
            o = jnp.dot(p.astype(BF16), vc, preferred_element_type=F32) / l
            o_ref[0, :, sl] = heads_merge(o).astype(BF16)

    @pl.when(t > 0)
    def _latent_queries():
        first_lane = lambda n: jnp.where(lax.broadcasted_iota(jnp.int32, (n, LANES), 1) == 0, 1.0, 0.0).astype(BF16)
        ones_w, ones_c = first_lane(NA_KH * GRID_W), first_lane(L)
        for rr in range(TM // GRID_W):
            r = (t - 1) * (TM // GRID_W) + rr
            rs = jnp.clip(r - NA_KH // 2, 0, rows - NA_KH)
            start = pl.multiple_of(L + rs * GRID_W, GRID_W)
            qrows = slice(rr * GRID_W, (rr + 1) * GRID_W)
            for jj, sl in enumerate(pairs):
                kc, vc = k_ref[0, 0:L, sl], v_ref[0, 0:L, sl]
                kw = k_ref[0, pl.ds(start, NA_KH * GRID_W), sl]
                vw = v_ref[0, pl.ds(start, NA_KH * GRID_W), sl]
                q2 = heads_stack(q_ref[0, qrows, sl])
                s_lat = lax.dot_general(q2, kw, NT_DIMS, preferred_element_type=F32) + bias_ref[jj, r - rs]
                s_ctx = lax.dot_general(q2, kc, NT_DIMS, preferred_element_type=F32)
                m = jnp.maximum(jnp.max(s_lat, axis=-1, keepdims=True), jnp.max(s_ctx, axis=-1, keepdims=True))
                p_lat = jnp.exp2(s_lat - m)
                p_ctx = jnp.exp2(s_ctx - m)
                oa = (jnp.dot(p_lat.astype(BF16), jnp.concatenate([vw, ones_w], axis=1), preferred_element_type=F32)
                      + jnp.dot(p_ctx.astype(BF16), jnp.concatenate([vc, ones_c], axis=1), preferred_element_type=F32))
                o = oa[:, :LANES] / oa[:, LANES:LANES + 1]
                o_ref[0, qrows, sl] = heads_merge(o).astype(BF16)


NA_PAIRS = 4


def na_attention(q, k, v, bias):
    B, Tt, _ = q.shape
    L = CTX_LEN
    rows = (Tt - L) // GRID_W
    w = NA_PAIRS * LANES
    return pl.pallas_call(
        functools.partial(_na_kernel, L=L, rows=rows),
        grid=(B, NA_W // w, Tt // TM),
        in_specs=[pl.BlockSpec((1, TM, w), lambda b, j, t: (b, t, j)),
                  pl.BlockSpec((1, Tt, w), lambda b, j, t: (b, 0, j)),
                  pl.BlockSpec((1, Tt, w), lambda b, j, t: (b, 0, j)),
                  pl.BlockSpec((NA_PAIRS,) + bias.shape[1:], lambda b, j, t: (j, 0, 0, 0),
                               pipeline_mode=pl.Buffered(1))],
        out_specs=pl.BlockSpec((1, TM, w), lambda b, j, t: (b, t, j)),
        out_shape=jax.ShapeDtypeStruct((B, Tt, NA_W), BF16),
        compiler_params=_cparams(3), name="na_attention",
    )(q, k, v, bias)


def na_bias_tables(rpb):
    depth, H = rpb.shape[:2]
    cq = jnp.arange(GRID_W)
    col_start = jnp.clip(cq - NA_KW // 2, 0, GRID_W - NA_KW)
    col_mask = (cq[None, :] >= col_start[:, None]) & (cq[None, :] < col_start[:, None] + NA_KW)
    col_idx = jnp.clip(cq[None, :] - cq[:, None] + NA_KW - 1, 0, 2 * NA_KW - 2)
    pick = (col_idx[None] == jnp.arange(2 * NA_KW - 1)[:, None, None]).astype(F32)
    cols = jnp.einsum('lhrc,cqk->lhrqk', rpb, pick, precision=lax.Precision.HIGHEST)
    cols = jnp.where(col_mask, cols, NEG)
    tbl = jnp.stack([cols[:, :, NA_KH - 1 - dl:2 * NA_KH - 1 - dl] for dl in range(NA_KH)], axis=2)
    tbl = jnp.transpose(tbl, (0, 1, 2, 4, 3, 5)).reshape(depth, H // 2, 2, NA_KH, GRID_W, NA_KH * GRID_W)
    return jnp.transpose(tbl, (0, 1, 3, 2, 4, 5)).reshape(depth, H // 2, NA_KH, 2 * GRID_W, NA_KH * GRID_W)


def _log_sigmoid(x):
    return jnp.minimum(x, 0.0) - jnp.log(1.0 + jnp.exp(-jnp.abs(x)))


def _mlstm_gate_kernel(gf_ref, gb_ref, rqf_ref, rqb_ref, col_ref, t_scr):
    nc, n = gf_ref.shape[1], M_CH
    rows = nc * 8
    lane = lax.broadcasted_iota(jnp.int32, (rows, n), 1)
    for d, (g_ref, rq_ref) in enumerate(((gf_ref, rqf_ref), (gb_ref, rqb_ref))):
        def scan(x, op, fill):
            sh = 1
            while sh < n:
                if d == 0:
                    x = op(x, jnp.where(lane >= sh, pltpu.roll(x, sh, 1), fill))
                else:
                    x = op(x, jnp.where(lane < n - sh, pltpu.roll(x, n - sh, 1), fill))
                sh *= 2
            return x

        top = g_ref[0, :, 0:8, :].reshape(rows, n)
        bot = g_ref[0, :, 8:16, :].reshape(rows, n)
        b = scan(_log_sigmoid(bot), jnp.add, 0.0)
        last = n - 1 if d == 0 else 0
        tot = jnp.broadcast_to(b[:, last:last + 1], (rows, n))
        r = top - b
        cm = scan(r, jnp.maximum, NEG)
        amax = jnp.broadcast_to(jnp.max(tot + r, axis=1, keepdims=True), (rows, n))
        for j, val in enumerate((r, cm, tot, amax)):
            rq_ref[0, :, 8 * j:8 * (j + 1), :] = val.reshape(nc, 8, n)
        t_scr[:, 16 * d:16 * d + 8, :] = b.reshape(nc, 8, n)
        t_scr[:, 16 * d + 8:16 * d + 16, :] = cm.reshape(nc, 8, n)

    def transpose_chunk(ci, carry):
        tile = jnp.concatenate([t_scr[ci], jnp.zeros((n - 32, n), F32)], axis=0)
        col_ref[0, ci] = tile.T
        return carry

    lax.fori_loop(0, nc, transpose_chunk, 0)


def mlstm_gates(gf, gb):
    B, nc, _, n = gf.shape
    blk = lambda r: pl.BlockSpec((1, nc, r, n), lambda b: (b, 0, 0, 0))
    return pl.pallas_call(
        _mlstm_gate_kernel,
        grid=(B,),
        in_specs=[blk(16), blk(16)],
        out_specs=[blk(32), blk(32), blk(n)],
        out_shape=[jax.ShapeDtypeStruct((B, nc, 32, n), F32)] * 2 + [jax.ShapeDtypeStruct((B, nc, n, n), F32)],
        scratch_shapes=[pltpu.VMEM((nc, 32, n), F32)],
        compiler_params=_cparams(1), name="mlstm_gates",
    )(gf, gb)


def _mlstm_chain_group(d, q_ref, k_ref, v_ref, rq_ref, col_ref, o_ref, ct_ref, m_ref, bb):
    n, H = M_CH, M_HEADS
    row8 = lax.broadcasted_iota(jnp.int32, (2 * H, n), 0)
    lane = lax.broadcasted_iota(jnp.int32, (n, n), 1)
    sub = lax.broadcasted_iota(jnp.int32, (n, n), 0)
    eye = lane == sub
    causal = (lane <= sub) if d == 0 else (lane >= sub)
    one0 = jnp.where(lane == 0, 1.0, 0.0).astype(BF16)

    r8, cm8 = rq_ref[bb, 0, 0:8, :], rq_ref[bb, 0, 8:16, :]
    tot, amax = rq_ref[bb, 0, 16:24, :], rq_ref[bb, 0, 24:32, :]
    m0 = m_ref[d, bb]
    nm8 = jnp.maximum(m0, cm8)
    m_new = jnp.maximum(tot + m0, amax)
    decay8 = jnp.exp(tot + m0 - m_new)
    w8 = jnp.exp(tot + r8 - m_new)
    e8 = jnp.exp(m0 - nm8)
    m_ref[d, bb] = jnp.where(row8 < H, m_new, 0.0)
    tc = col_ref[bb, 0]

    for hh in range(H):
        sl = slice(hh * M_HEAD_DIM, (hh + 1) * M_HEAD_DIM)
        qh, kh, vh = q_ref[bb, :, sl], k_ref[bb, :, sl], v_ref[bb, :, sl]
        b_col = tc[:, 16 * d + hh:16 * d + hh + 1]
        nm_col = jnp.maximum(m0[hh:hh + 1, 0:1], tc[:, 16 * d + 8 + hh:16 * d + 8 + hh + 1])
        dmat = jnp.where(causal, jnp.exp(r8[hh:hh + 1, :] - nm_col), 0.0)
        s = (lax.dot_general(qh, kh, NT_DIMS, preferred_element_type=F32) * dmat).astype(BF16)
        e_diag = jnp.where(eye, e8[hh:hh + 1, :], 0.0).astype(BF16)
        w_diag = jnp.where(eye, w8[hh:hh + 1, :], 0.0).astype(BF16)
        vaug = jnp.concatenate([vh, one0], axis=1)
        ct = ct_ref[d, bb, hh]
        inter = jnp.dot(qh, ct.astype(BF16), preferred_element_type=F32).astype(BF16)
        num = jnp.dot(jnp.concatenate([s, e_diag], axis=1), jnp.concatenate([vaug, inter], axis=0),
                      preferred_element_type=F32)
        den = num[:, M_HEAD_DIM:M_HEAD_DIM + 1]
        o_ref[bb, :, sl] = num[:, :M_HEAD_DIM] / jnp.maximum(jnp.abs(den), jnp.exp(-(b_col + nm_col)))
        wv = jnp.dot(w_diag, vaug, preferred_element_type=F32).astype(BF16)
        ct_ref[d, bb, hh] = decay8[hh:hh + 1, 0:1] * ct + lax.dot_general(kh, wv, TN_DIMS, preferred_element_type=F32)


def _mlstm_kernel(qf_ref, kf_ref, vf_ref, rqf_ref, colf_ref, qb_ref, kb_ref, vb_ref, rqb_ref, colb_ref,
                  of_ref, ob_ref, ct_ref, m_ref):
    @pl.when(pl.program_id(1) == 0)
    def _():
        ct_ref[...] = jnp.zeros_like(ct_ref)
        m_ref[...] = jnp.zeros_like(m_ref)

    for bb in range(qf_ref.shape[0]):
        _mlstm_chain_group(0, qf_ref, kf_ref, vf_ref, rqf_ref, colf_ref, of_ref, ct_ref, m_ref, bb)
        _mlstm_chain_group(1, qb_ref, kb_ref, vb_ref, rqb_ref, colb_ref, ob_ref, ct_ref, m_ref, bb)


M_BATCH = 4


def mlstm(q, k, v, gf, gb):
    B, Tt, _ = q.shape
    nc = Tt // M_CH
    lc = CTX_LEN // M_CH
    nb = M_BATCH if B % M_BATCH == 0 else 1
    rqf, rqb, col = mlstm_gates(gf, gb)

    def bw(s):
        return jnp.where(s < lc, lc - 1 - s, nc - 1 + lc - s)

    tok_f = pl.BlockSpec((nb, M_CH, M_W), lambda b, s: (b, s, 0))
    tok_b = pl.BlockSpec((nb, M_CH, M_W), lambda b, s: (b, bw(s), 0))
    chunk_f = lambda r: pl.BlockSpec((nb, 1, r, M_CH), lambda b, s: (b, s, 0, 0))
    chunk_b = lambda r: pl.BlockSpec((nb, 1, r, M_CH), lambda b, s: (b, bw(s), 0, 0))
    return pl.pallas_call(
        _mlstm_kernel,
        grid=(B // nb, nc),
        in_specs=[tok_f, tok_f, tok_f, chunk_f(32), chunk_f(M_CH), tok_b, tok_b, tok_b, chunk_b(32), chunk_b(M_CH)],
        out_specs=[tok_f, tok_b],
        out_shape=[jax.ShapeDtypeStruct((B, Tt, M_W), F32)] * 2,
        scratch_shapes=[pltpu.VMEM((2, nb, M_HEADS, M_HEAD_DIM, M_HEAD_DIM + LANES), F32),
                        pltpu.VMEM((2, nb, 2 * M_HEADS, M_CH), F32)],
        compiler_params=_cparams(2), name="mlstm",
    )(q, k, v, rqf, col, q, k, v, rqb, col)


def _conv_tile(t, nt, cur_ref, prev_ref, next_ref, w_ref, b_ref, g_ref, be_ref, o_ref, hbuf, hsh):
    C = CONV_CH

    def glu(u):
        u = u.astype(F32)
        return u[:, :C] * _sigmoid(u[:, C:])

    prev_ok = t >= 2
    next_ok = jnp.logical_and(t >= 1, t < nt - 1)
    hbuf[0:CONV_HALO] = jnp.where(prev_ok, glu(prev_ref[0]), 0.0)
    hbuf[CONV_HALO:CONV_HALO + TM] = glu(cur_ref[0])
    hbuf[CONV_HALO + TM:2 * CONV_HALO + TM] = jnp.where(next_ok, glu(next_ref[0]), 0.0)
    span = TM + 2 * CONV_HALO - 8
    for r in range(8):
        hsh[r] = hbuf[r:r + span, :]
    off = CONV_HALO - CONV_K // 2
    for rc in range(TM // CONV_ROWS):
        acc = jnp.zeros((CONV_ROWS, C), F32) + b_ref[...]
        for j in range(CONV_K):
            a, r = divmod(off + j, 8)
            acc = acc + hsh[r, rc * CONV_ROWS + 8 * a:rc * CONV_ROWS + 8 * a + CONV_ROWS, :] * w_ref[j:j + 1, :]
        mu = jnp.mean(acc, axis=-1, keepdims=True)
        xc = acc - mu
        var = jnp.mean(xc * xc, axis=-1, keepdims=True)
        y = xc * lax.rsqrt(var + NORM_EPS) * g_ref[...] + be_ref[...]
        o_ref[rc * CONV_ROWS:(rc + 1) * CONV_ROWS, :] = (y * _sigmoid(y)).astype(BF16)


def _merge_kernel(na_ref, hfw_ref, hbw_ref, mo_ref, cu_ref, cup_ref, cun_ref, bg_ref, x_ref, mod_ref, ng_ref,
                  cw_ref, cb_ref, cg_ref, cbe_ref, wna_ref, wm_ref, wc_ref, wo_ref, wr_ref,
                  x1_ref, h2_ref, aff_ref, cv_ref, hbuf, hsh, *, D, nt):
    _conv_tile(pl.program_id(1), nt, cu_ref, cup_ref, cun_ref, cw_ref, cb_ref, cg_ref, cbe_ref, cv_ref, hbuf, hsh)
    mod = mod_ref[0, 0]
    mh = hfw_ref[0] + hbw_ref[0]
    parts = []
    for hh in range(M_HEADS):
        xh = mh[:, hh * M_HEAD_DIM:(hh + 1) * M_HEAD_DIM]
        parts.append(xh * lax.rsqrt(jnp.mean(xh * xh, axis=-1, keepdims=True) + NORM_EPS))
    mn = (jnp.concatenate(parts, axis=1) * ng_ref[...] * _sigmoid(mo_ref[0].astype(F32))).astype(BF16)
    z = (_sigmoid(bg_ref[0, :, 0:D].astype(F32)) * jnp.dot(na_ref[0], wna_ref[...], preferred_element_type=F32)
         + _sigmoid(bg_ref[0, :, D:2 * D].astype(F32)) * jnp.dot(mn, wm_ref[...], preferred_element_type=F32)
         + _sigmoid(bg_ref[0, :, 2 * D:3 * D].astype(F32)) * jnp.dot(cv_ref[...], wc_ref[...], preferred_element_type=F32))
    y = jnp.dot(z.astype(BF16), wo_ref[...], preferred_element_type=F32)
    x1 = x_ref[0] + mod[:, 2 * D:3 * D] * y
    x1_ref[0] = x1
    h2 = _modnorm(x1, mod[:, 3 * D:4 * D], mod[:, 4 * D:5 * D]).astype(BF16)
    h2_ref[0] = h2
    logits = lax.dot_general(wr_ref[...], h2, NT_DIMS, preferred_element_type=F32)
    e = jnp.exp(logits - jnp.max(logits, axis=0, keepdims=True))
    aff_ref[0] = e / jnp.sum(e, axis=0, keepdims=True)


def merge_and_route(na, hfw, hbw, mo, cu, bg, X, modsel, norm_gain, conv_w, conv_b, ln_g, ln_b,
                    w_na, w_m, w_c, w_o, w_rt):
    B, Tt, D = X.shape
    E = w_rt.shape[0]
    C = CONV_CH
    nt = Tt // TM
    hb = TM // CONV_HALO
    nhb = Tt // CONV_HALO
    wpad = jnp.zeros((32, C), F32).at[:CONV_K].set(conv_w)
    vec = lambda a: a.reshape(1, C)
    tok = lambda w: pl.BlockSpec((1, TM, w), lambda b, t: (b, t, 0))
    const = lambda shape: pl.BlockSpec(shape, lambda b, t: (0,) * len(shape))
    return pl.pallas_call(
        functools.partial(_merge_kernel, D=D, nt=nt),
        grid=(B, nt),
        in_specs=[tok(NA_W), tok(M_W), tok(M_W), tok(M_W), tok(2 * C),
                  pl.BlockSpec((1, CONV_HALO, 2 * C), lambda b, t: (b, jnp.maximum(t * hb - 1, 0), 0)),
                  pl.BlockSpec((1, CONV_HALO, 2 * C), lambda b, t: (b, jnp.minimum((t + 1) * hb, nhb - 1), 0)),
                  tok(3 * D), tok(D),
                  pl.BlockSpec((1, 1, 1, 6 * D), lambda b, t: (b, jnp.minimum(t, 1), 0, 0)),
                  const((1, M_W)), const((32, C)), const((1, C)), const((1, C)), const((1, C)),
                  const((NA_W, D)), const((M_W, D)), const((C, D)), const((D, D)), const((E, D))],
        out_specs=[tok(D), tok(D), pl.BlockSpec((1, E, TM), lambda b, t: (b, 0, t))],
        out_shape=[jax.ShapeDtypeStruct((B, Tt, D), F32), jax.ShapeDtypeStruct((B, Tt, D), BF16),
                   jax.ShapeDtypeStruct((B, E, Tt), F32)],
        scratch_shapes=[pltpu.VMEM((TM, C), BF16), pltpu.VMEM((TM + 2 * CONV_HALO, C), F32),
                        pltpu.VMEM((8, TM + 2 * CONV_HALO - 8, C), F32)],
        compiler_params=_cparams(2), name="merge_and_route",
    )(na, hfw, hbw, mo, cu, cu, cu, bg, X, modsel, norm_gain, wpad, vec(conv_b), vec(ln_g), vec(ln_b),
      w_na, w_m, w_c, w_o, w_rt)


def _topk_kernel(aff_ref, pos_ref, cnt_ref, *, L, k_ctx, k_lat):
    E, Tt = aff_ref.shape[1:]
    u = pltpu.bitcast(aff_ref[0], jnp.int32)
    is_ctx = lax.broadcasted_iota(jnp.int32, (E, Tt), 1) < L

    def count(mask):
        f = jnp.where(mask, 1.0, 0.0)
        c_ctx = jnp.sum(jnp.where(is_ctx, f, 0.0), axis=1, keepdims=True)
        return c_ctx, jnp.sum(f, axis=1, keepdims=True) - c_ctx

    def bit_step(i, carry):
        v_ctx, v_lat = carry
        bit = jnp.left_shift(jnp.int32(1), 30 - i)
        c_ctx, c_lat = count(u >= jnp.where(is_ctx, v_ctx | bit, v_lat | bit))
        return (jnp.where(c_ctx >= k_ctx, v_ctx | bit, v_ctx), jnp.where(c_lat >= k_lat, v_lat | bit, v_lat))

    zero = jnp.zeros((E, 1), jnp.int32)
    v_ctx, v_lat = lax.fori_loop(0, 31, bit_step, (zero, zero))
    thr = jnp.where(is_ctx, v_ctx, v_lat)
    g_ctx, g_lat = count(u > thr)
    need_ctx, need_lat = k_ctx - g_ctx, k_lat - g_lat

    r = lax.broadcasted_iota(jnp.int32, (TM, TM), 0)
    c = lax.broadcasted_iota(jnp.int32, (TM, TM), 1)
    tri = jnp.where(r < c, 1.0, 0.0).astype(BF16)
    lane_t = lax.broadcasted_iota(jnp.int32, (E, LANES), 1)
    cnt = jnp.zeros((E, LANES), F32)
    ties = jnp.zeros((E, 1), F32)
    slots = jnp.zeros((E, 1), F32)
    nt = Tt // TM
    for t in range(nt):
        if t == 1:
            ties = jnp.zeros((E, 1), F32)
        need = need_ctx if t == 0 else need_lat
        ut = pltpu.bitcast(aff_ref[0, :, t * TM:(t + 1) * TM], jnp.int32)
        tt = v_ctx if t == 0 else v_lat
        eq = jnp.where(ut == tt, 1.0, 0.0)
        rank = ties + jnp.dot(eq.astype(BF16), tri, preferred_element_type=F32)
        sel = jnp.where(ut > tt, 1.0, jnp.where(rank < need, eq, 0.0))
        pos = slots + jnp.dot(sel.astype(BF16), tri, preferred_element_type=F32)
        pos_ref[0, :, t * TM:(t + 1) * TM] = jnp.where(sel > 0.0, pos, -1.0).astype(jnp.int32)
        cnt = jnp.where(lane_t == t, slots, cnt)
        ties = ties + jnp.sum(eq, axis=1, keepdims=True)
        slots = slots + jnp.sum(sel, axis=1, keepdims=True)
    cnt = jnp.where(lane_t == nt, slots, cnt)
    cnt_ref[0] = cnt.astype(jnp.int32)


def expert_choice_slots(aff):
    B, E, Tt = aff.shape
    L = CTX_LEN
    k_ctx = EC_FACTOR * L // N_EXPERTS
    k_lat = EC_FACTOR * (Tt - L) // N_EXPERTS
    assert Tt // TM + 1 <= LANES
    return pl.pallas_call(
        functools.partial(_topk_kernel, L=L, k_ctx=k_ctx, k_lat=k_lat),
        grid=(B,),
        in_specs=[pl.BlockSpec((1, E, Tt), lambda b: (b, 0, 0))],
        out_specs=[pl.BlockSpec((1, E, Tt), lambda b: (b, 0, 0)), pl.BlockSpec((1, E, LANES), lambda b: (b, 0, 0))],
        out_shape=[jax.ShapeDtypeStruct((B, E, Tt), jnp.int32), jax.ShapeDtypeStruct((B, E, LANES), jnp.int32)],
        compiler_params=_cparams(1), name="expert_choice_slots",
    )(aff)


def _ffn_kernel(cnt_ref, h_ref, pos_ref, wg_ref, wu_ref, wd_ref, y_ref, xin_ref, acc_ref, *, ns, nt, unroll):
    b, e, f = pl.program_id(0), pl.program_id(1), pl.program_id(2)
    W = SLOT_WIN
    G = xin_ref.shape[0]
    eg = lax.rem(e, G)

    @pl.when(f == 0)
    def _():
        acc_ref[...] = jnp.zeros_like(acc_ref)

    @pl.when(jnp.logical_and(f == 0, eg == 0))
    def _gather():
        xin_ref[...] = jnp.zeros_like(xin_ref)

        def onehot(q, t, base):
            srow = base + lax.broadcasted_iota(jnp.int32, (W, TM), 0)
            return jnp.where(pos_ref[0, q, :, pl.ds(pl.multiple_of(t * TM, TM), TM)] == srow, 1.0, 0.0).astype(BF16)

        def rows_of(t, onehots):
            return jnp.dot(onehots, h_ref[0, pl.ds(pl.multiple_of(t * TM, TM), TM), :], preferred_element_type=F32)

        def add_rows(q, base, rows):
            xin_ref[q, pl.ds(base, W), :] = xin_ref[q, pl.ds(base, W), :] + rows.astype(BF16)

        def group_body(g, carry):
            tiles = [g * unroll + k for k in range(unroll)]
            bases = [[pl.multiple_of((cnt_ref[b, e + q, t] // 16) * 16, 16) for q in range(G)] for t in tiles]
            first = [rows_of(t, jnp.concatenate([onehot(q, t, bs[q]) for q in range(G)], axis=0))
                     for t, bs in zip(tiles, bases)]
            for t, bs, rows in zip(tiles, bases, first):
                for q in range(G):
                    add_rows(q, bs[q], rows[q * W:(q + 1) * W])

                    def more(r, c2, t=t, q=q, base=bs[q]):
                        nxt = pl.multiple_of(base + r * W, 16)
                        add_rows(q, nxt, rows_of(t, onehot(q, t, nxt)))
                        return c2

                    lax.fori_loop(1, (cnt_ref[b, e + q, t + 1] - bs[q] + W - 1) // W, more, 0)
            return carry

        lax.fori_loop(0, nt // unroll, group_body, 0)

    x = xin_ref[eg, 0:ns, :]
    a = jnp.dot(x, wg_ref[0, 0].astype(BF16), preferred_element_type=F32)
    u = jnp.dot(x, wu_ref[0, 0].astype(BF16), preferred_element_type=F32)
    hmid = (a * _sigmoid(a) * u).astype(BF16)
    acc_ref[...] += jnp.dot(hmid, wd_ref[0, 0].astype(BF16), preferred_element_type=F32)

    @pl.when(f == pl.num_programs(2) - 1)
    def _():
        y_ref[0, 0, 0:ns, :] = acc_ref[...].astype(BF16)
        y_ref[0, 0, ns:ns + SLOT_WIN, :] = jnp.zeros((SLOT_WIN, y_ref.shape[-1]), BF16)


FFN_GATHER_EXPERTS = 4


def expert_ffn(cnt, h2, pos, w_gate, w_up, w_down, layer):
    B, Tt, D = h2.shape
    _, E, _, F = w_gate.shape
    L = CTX_LEN
    ns = EC_FACTOR * L // N_EXPERTS + EC_FACTOR * (Tt - L) // N_EXPERTS
    assert ns % 16 == 0
    tf = min(512, F)
    nt = Tt // TM
    unroll = max(u for u in range(1, 4) if nt % u == 0)
    G = FFN_GATHER_EXPERTS
    assert E % G == 0
    grid_spec = pltpu.PrefetchScalarGridSpec(
        num_scalar_prefetch=1,
        grid=(B, E, F // tf),
        in_specs=[pl.BlockSpec((1, Tt, D), lambda b, e, f, c: (b, 0, 0), pipeline_mode=pl.Buffered(1)),
                  pl.BlockSpec((1, G, 1, Tt), lambda b, e, f, c: (b, e // G, 0, 0)),
                  pl.BlockSpec((1, 1, D, tf), lambda b, e, f, c: (layer, e, 0, f)),
                  pl.BlockSpec((1, 1, D, tf), lambda b, e, f, c: (layer, e, 0, f)),
                  pl.BlockSpec((1, 1, tf, D), lambda b, e, f, c: (layer, e, f, 0))],
        out_specs=pl.BlockSpec((1, 1, ns + SLOT_WIN, D), lambda b, e, f, c: (b, e, 0, 0)),
        scratch_shapes=[pltpu.VMEM((G, ns + SLOT_WIN, D), BF16), pltpu.VMEM((ns, D), F32)])
    return pl.pallas_call(
        functools.partial(_ffn_kernel, ns=ns, nt=nt, unroll=unroll),
        grid_spec=grid_spec,
        out_shape=jax.ShapeDtypeStruct((B, E, ns + SLOT_WIN, D), BF16),
        compiler_params=_cparams(3), name="expert_ffn",
    )(cnt, h2, pos.reshape(B, E, 1, Tt), w_gate, w_up, w_down)


def _combine_kernel(cnt_ref, pos_ref, aff_ref, y_ref, x_ref, mod_ref, o_ref, yw_ref, p_ref, acc_ref, *, D, ns, t0):
    b, t = pl.program_id(0), pl.program_id(1) + t0
    E = pos_ref.shape[1]
    W = SLOT_WIN
    starts, rounds = [], jnp.int32(0)
    for e in range(E):
        lo, hi = cnt_ref[b, e, t], cnt_ref[b, e, t + 1]
        s0 = (lo // 16) * 16
        starts.append(s0)
        rounds = jnp.maximum(rounds, jnp.where(hi > lo, (hi - s0 + W - 1) // W, 0))
    def round_sum(r):
        for e in range(E):
            base = pl.multiple_of(jnp.minimum(starts[e] + r * W, ns), 16)
            yw_ref[e * W:(e + 1) * W, :] = y_ref[0, e, pl.ds(base, W), :]
            srow = base + lax.broadcasted_iota(jnp.int32, (W, TM), 0)
            p_ref[e * W:(e + 1) * W, :] = jnp.where(pos_ref[0, e:e + 1, :] == srow, aff_ref[0, e:e + 1, :], 0.0).astype(BF16)
        return lax.dot_general(p_ref[...], yw_ref[...], TN_DIMS, preferred_element_type=F32)

    acc_ref[...] = round_sum(0)

    def extra_round(r, carry):
        acc_ref[...] += round_sum(r)
        return carry

    lax.fori_loop(1, rounds, extra_round, 0)
    o_ref[0] = x_ref[0] + mod_ref[0, 0][:, 5 * D:6 * D] * acc_ref[...]


def combine(cnt, pos, aff, Y, X1, modsel, latent_only):
    B, Tt, D = X1.shape
    E = pos.shape[1]
    nsp = Y.shape[2]
    t0 = 1 if latent_only else 0
    grid_spec = pltpu.PrefetchScalarGridSpec(
        num_scalar_prefetch=1,
        grid=(B, Tt // TM - t0),
        in_specs=[pl.BlockSpec((1, E, TM), lambda b, t, c: (b, 0, t + t0)),
                  pl.BlockSpec((1, E, TM), lambda b, t, c: (b, 0, t + t0)),
                  pl.BlockSpec((1, E, nsp, D), lambda b, t, c: (b, 0, 0, 0), pipeline_mode=pl.Buffered(1)),
                  pl.BlockSpec((1, TM, D), lambda b, t, c: (b, t + t0, 0)),
                  pl.BlockSpec((1, 1, 1, 6 * D), lambda b, t, c: (b, jnp.minimum(t + t0, 1), 0, 0))],
        out_specs=pl.BlockSpec((1, TM, D), lambda b, t, c: (b, t, 0)),
        scratch_shapes=[pltpu.VMEM((E * SLOT_WIN, D), BF16), pltpu.VMEM((E * SLOT_WIN, TM), BF16),
                        pltpu.VMEM((TM, D), F32)])
    return pl.pallas_call(
        functools.partial(_combine_kernel, D=D, ns=nsp - SLOT_WIN, t0=t0),
        grid_spec=grid_spec,
        out_shape=jax.ShapeDtypeStruct((B, Tt - t0 * TM, D), F32),
        compiler_params=_cparams(2), name="combine",
    )(cnt, pos, aff, Y, X1, modsel)


def _pack_w_in(w, D):
    widths = (NA_W, NA_W, NA_W, M_W, M_W, M_W, M_W, N_GATE_COLS, 2 * CONV_CH, 3 * D)
    offs = [0]
    for wd in widths:
        offs.append(offs[-1] + wd)
    seg = [w[:, offs[i]:offs[i + 1]] for i in range(len(widths))]
    return jnp.concatenate(seg[:7] + seg[8:], axis=1).astype(BF16), _gate_rows(seg[7].T).astype(BF16)


def _gate_rows(g):
    H = M_HEADS
    i_fw, f_fw, i_bw, f_bw = (g[j * H:(j + 1) * H] for j in range(4))
    return jnp.concatenate([i_fw, f_fw, f_fw, i_fw, i_bw, f_bw, f_bw, i_bw], axis=0)


def _rope_tables(T, L):
    nf = M_HEAD_DIM // 4
    t = jnp.arange(T)
    row = (t // GRID_W).astype(F32)
    col = (t % GRID_W).astype(F32)
    inv = ROPE_BASE ** (-jnp.arange(nf, dtype=F32) / nf)
    ar, ac = row[:, None] * inv[None, :], col[:, None] * inv[None, :]
    z = jnp.zeros((T, nf), F32)
    cos = jnp.concatenate([jnp.cos(ar), jnp.cos(ar), jnp.cos(ac), jnp.cos(ac)], axis=1)
    sina = jnp.concatenate([-jnp.sin(ar), z, -jnp.sin(ac), z], axis=1)
    sinb = jnp.concatenate([z, jnp.sin(ar), z, jnp.sin(ac)], axis=1)
    ident = lambda v: jnp.full((L, M_HEAD_DIM), v, F32)
    return (jnp.concatenate([ident(1.0), cos]), jnp.concatenate([ident(0.0), sina]),
            jnp.concatenate([ident(0.0), sinb]))


def kernel(x, c, ctx, c_ctx, w_mod, b_mod, w_in, na_q_gain, na_k_gain, na_rpb, m_gate_bias,
           m_norm_gain, conv_w, conv_b, conv_ln_g, conv_ln_b, w_br_na, w_br_m, w_br_conv, w_out,
           w_router, w_gate, w_up, w_down):
    B, T, D = x.shape
    L = ctx.shape[1]
    assert L == CTX_LEN == TM and T % TM == 0 and B + 1 <= 16
    depth = w_in.shape[0]
    X = jnp.concatenate([ctx, x], axis=1)
    cc = jnp.zeros((16, D), F32).at[:B].set(c).at[B].set(c_ctx)
    mods = modulation(cc, w_mod, b_mod)
    cos, sina, sinb = _rope_tables(T, L)
    hd = jnp.arange(NA_W) // NA_HEAD_DIM
    gmat = (hd[:, None] == hd[None, :]).astype(BF16)
    bias_tables = na_bias_tables(na_rpb * LOG2E)
    for l in range(depth):
        modsel = jnp.stack([jnp.broadcast_to(mods[l, B], (B, 6 * D)), mods[l, :B]], axis=1).reshape(B, 2, 1, 6 * D)
        q_gain = (jnp.tile(na_q_gain[l], NA_HEADS) * (NA_HEAD_DIM ** -0.5 * LOG2E)).reshape(1, NA_W)
        k_gain = jnp.tile(na_k_gain[l], NA_HEADS).reshape(1, NA_W)
        w_tok, w_gates_t = _pack_w_in(w_in[l], D)
        naq, nak, nav, mq, mk, mv, mo, mgf, mgb, cu, bg = input_projection(
            X, modsel, w_tok, q_gain, k_gain, gmat, cos, sina, sinb, w_gates_t,
            _gate_rows(m_gate_bias[l].reshape(N_GATE_COLS, 1)))
        na = na_attention(naq, nak, nav, bias_tables[l])
        hfw, hbw = mlstm(mq, mk, mv, mgf, mgb)
        X1, h2, aff = merge_and_route(
            na, hfw, hbw, mo, cu, bg, X, modsel, m_norm_gain[l].reshape(1, M_W),
            conv_w[l], conv_b[l], conv_ln_g[l], conv_ln_b[l],
            w_br_na[l].astype(BF16), w_br_m[l].astype(BF16), w_br_conv[l].astype(BF16), w_out[l].astype(BF16),
            w_router[l].T.astype(BF16))
        pos, cnt = expert_choice_slots(aff)
        cnt = cnt[:, :, :T // TM + 2]
        Y = expert_ffn(cnt, h2, pos, w_gate, w_up, w_down, l)
        X = combine(cnt, pos, aff, Y, X1, modsel, latent_only=(l == depth - 1))
    return X
```

```python
import functools

import jax
import jax.numpy as jnp
from jax import lax
from jax.experimental import pallas as pl
from jax.experimental.pallas import tpu as pltpu

DEPTH = 4
CTX_LEN = 256
GRID_W = 64
NA_HEADS = 8
NA_HEAD_DIM = 64
NA_KH = 8
NA_KW = 16
M_HEADS = 4
M_HEAD_DIM = 128
CONV_CH = 512
CONV_K = 31
N_EXPERTS = 16
EC_FACTOR = 2
ROPE_BASE = 10000.0
NORM_EPS = 1e-6

NA_W = NA_HEADS * NA_HEAD_DIM
M_W = M_HEADS * M_HEAD_DIM
N_GATE_COLS = 4 * M_HEADS

LANES = 128
TM = 256
M_CH = 128
CONV_HALO = 16
CONV_ROWS = 64
SLOT_WIN = 64
NEG = -1e30
LOG2E = 1.4426950408889634
VMEM_LIMIT_BYTES = 56 * 1024 * 1024

F32 = jnp.float32
BF16 = jnp.bfloat16
NT_DIMS = (((1,), (1,)), ((), ()))
TN_DIMS = (((0,), (0,)), ((), ()))


def _cparams(n_axes):
    return pltpu.CompilerParams(dimension_semantics=("arbitrary",) * n_axes,
                                vmem_limit_bytes=VMEM_LIMIT_BYTES)


def _sigmoid(x):
    return 1.0 / (1.0 + jnp.exp(-x))


def _modnorm(x, shift, scale):
    ms = jnp.mean(x * x, axis=-1, keepdims=True)
    return (x * lax.rsqrt(ms + NORM_EPS)) * (1.0 + scale) + shift


def _chunk(width):
    return 512 if width % 512 == 0 else 256


def _mod_kernel(c_ref, w_ref, b_ref, o_ref):
    c = c_ref[...]
    s = (c * _sigmoid(c)).astype(BF16)
    o_ref[0] = jnp.dot(s, w_ref[0].astype(BF16), preferred_element_type=F32) + b_ref[0]


def modulation(cc, w_mod, b_mod):
    depth, D, N = w_mod.shape
    tn = 1024 if N % 1024 == 0 else N
    return pl.pallas_call(
        _mod_kernel,
        grid=(depth, N // tn),
        in_specs=[pl.BlockSpec((16, D), lambda l, n: (0, 0)),
                  pl.BlockSpec((1, D, tn), lambda l, n: (l, 0, n)),
                  pl.BlockSpec((1, 1, tn), lambda l, n: (l, 0, n))],
        out_specs=pl.BlockSpec((1, 16, tn), lambda l, n: (l, 0, n)),
        out_shape=jax.ShapeDtypeStruct((depth, 16, N), F32),
        compiler_params=_cparams(2), name="modulation",
    )(cc, w_mod, b_mod.reshape(depth, 1, N))


def _inproj_kernel(x_ref, mod_ref, w_ref, qg_ref, kg_ref, gm_ref, cos_ref, sa_ref, sb_ref, wgt_ref, gb_ref,
                   naq_ref, nak_ref, nav_ref, mq_ref, mk_ref, mv_ref, mo_ref, mgf_ref, mgb_ref, cu_ref, bg_ref, *, D):
    mod = mod_ref[0, 0]
    h = _modnorm(x_ref[0], mod[:, 0:D], mod[:, D:2 * D]).astype(BF16)

    def proj(off, width):
        return jnp.dot(h, w_ref[:, off:off + width], preferred_element_type=F32)

    gm = gm_ref[...]
    for ref, gain_ref, off in ((naq_ref, qg_ref, 0), (nak_ref, kg_ref, NA_W)):
        y = proj(off, NA_W)
        ysq = y * y
        hi = ysq.astype(BF16)
        lo = (ysq - hi.astype(F32)).astype(BF16)
        ss = jnp.dot(hi, gm, preferred_element_type=F32) + jnp.dot(lo, gm, preferred_element_type=F32)
        ref[0] = (y * lax.rsqrt(ss * (1.0 / NA_HEAD_DIM) + NORM_EPS) * gain_ref[...]).astype(BF16)
    nav_ref[0] = proj(2 * NA_W, NA_W).astype(BF16)

    cos, sa, sb = cos_ref[...], sa_ref[...], sb_ref[...]
    q4 = M_HEAD_DIM // 4
    base = 3 * NA_W
    for ref, off, scale in ((mq_ref, base, 1.0), (mk_ref, base + M_W, M_HEAD_DIM ** -0.5)):
        y = proj(off, M_W)
        for hh in range(M_HEADS):
            yh = y[:, hh * M_HEAD_DIM:(hh + 1) * M_HEAD_DIM]
            r = yh * cos + pltpu.roll(yh, M_HEAD_DIM - q4, 1) * sa + pltpu.roll(yh, q4, 1) * sb
            ref[0, :, hh * M_HEAD_DIM:(hh + 1) * M_HEAD_DIM] = (r * scale).astype(BF16)
    mv_ref[0] = proj(base + 2 * M_W, M_W).astype(BF16)
    mo_ref[0] = proj(base + 3 * M_W, M_W).astype(BF16)
    base += 4 * M_W
    for ref, width in ((cu_ref, 2 * CONV_CH), (bg_ref, 3 * D)):
        cw = _chunk(width)
        for j in range(width // cw):
            ref[0, :, j * cw:(j + 1) * cw] = proj(base + j * cw, cw).astype(BF16)
        base += width
    g = lax.dot_general(wgt_ref[...], h, NT_DIMS, preferred_element_type=F32) + gb_ref[...]
    nrow = g.shape[0] // 2
    for cidx in range(g.shape[1] // M_CH):
        mgf_ref[0, cidx] = g[0:nrow, cidx * M_CH:(cidx + 1) * M_CH]
        mgb_ref[0, cidx] = g[nrow:2 * nrow, cidx * M_CH:(cidx + 1) * M_CH]


def input_projection(X, modsel, w_packed, q_gain, k_gain, gmat, cos, sina, sinb, w_gates_t, gate_bias):
    B, Tt, D = X.shape
    NP = w_packed.shape[1]
    nt = Tt // TM
    tok = lambda w: pl.BlockSpec((1, TM, w), lambda b, t: (b, t, 0))
    const = lambda shape: pl.BlockSpec(shape, lambda b, t: (0,) * len(shape))
    out_widths = (NA_W, NA_W, NA_W, M_W, M_W, M_W, M_W, None, None, 2 * CONV_CH, 3 * D)
    gate_rows = w_gates_t.shape[0] // 2
    gate_spec = pl.BlockSpec((1, TM // M_CH, gate_rows, M_CH), lambda b, t: (b, t, 0, 0))
    return pl.pallas_call(
        functools.partial(_inproj_kernel, D=D),
        grid=(B, nt),
        in_specs=[tok(D),
                  pl.BlockSpec((1, 1, 1, 6 * D), lambda b, t: (b, jnp.minimum(t, 1), 0, 0)),
                  pl.BlockSpec((D, NP), lambda b, t: (0, 0), pipeline_mode=pl.Buffered(1)),
                  const((1, NA_W)), const((1, NA_W)), const((NA_W, NA_W)),
                  pl.BlockSpec((TM, M_HEAD_DIM), lambda b, t: (t, 0)),
                  pl.BlockSpec((TM, M_HEAD_DIM), lambda b, t: (t, 0)),
                  pl.BlockSpec((TM, M_HEAD_DIM), lambda b, t: (t, 0)),
                  const((2 * gate_rows, D)), const((2 * gate_rows, 1))],
        out_specs=[gate_spec if w is None else tok(w) for w in out_widths],
        out_shape=[jax.ShapeDtypeStruct((B, Tt // M_CH, gate_rows, M_CH), F32) if w is None
                   else jax.ShapeDtypeStruct((B, Tt, w), BF16) for w in out_widths],
        compiler_params=_cparams(2), name="input_projection",
    )(X, modsel, w_packed, q_gain, k_gain, gmat, cos, sina, sinb, w_gates_t, gate_bias)


def _na_kernel(q_ref, k_ref, v_ref, bias_ref, o_ref, *, L, rows):
    t = pl.program_id(2)
    even = lax.broadcasted_iota(jnp.int32, (1, LANES), 1) < NA_HEAD_DIM
    pairs = [slice(jj * LANES, (jj + 1) * LANES) for jj in range(NA_PAIRS)]

    def heads_stack(q):
        z = jnp.zeros_like(q)
        return jnp.concatenate([jnp.where(even, q, z), jnp.where(even, z, q)], axis=0)

    def heads_merge(o):
        n = o.shape[0] // 2
        return jnp.where(even, o[:n], o[n:])

    @pl.when(t == 0)
    def _context_queries():
        for sl in pairs:
            kc, vc = k_ref[0, 0:L, sl], v_ref[0, 0:L, sl]
            q2 = heads_stack(q_ref[0, :, sl])
            s = lax.dot_general(q2, kc, NT_DIMS, preferred_element_type=F32)
            m = jnp.max(s, axis=-1, keepdims=True)
            p = jnp.exp2(s - m)
            l = jnp.sum(p, axis=-1, keepdims=True)
            o = jnp.dot(p.astype(BF16), vc, preferred_element_type=F32) / l
            o_ref[0, :, sl] = heads_merge(o).astype(BF16)

    @pl.when(t > 0)
    def _latent_queries():
        first_lane = lambda n: jnp.where(lax.broadcasted_iota(jnp.int32, (n, LANES), 1) == 0, 1.0, 0.0).astype(BF16)
        ones_w, ones_c = first_lane(NA_KH * GRID_W), first_lane(L)
        for rr in range(TM // GRID_W):
            r = (t - 1) * (TM // GRID_W) + rr
            rs = jnp.clip(r - NA_KH // 2, 0, rows - NA_KH)
            start = pl.multiple_of(L + rs * GRID_W, GRID_W)
            qrows = slice(rr * GRID_W, (rr + 1) * GRID_W)
            for jj, sl in enumerate(pairs):
                kc, vc = k_ref[0, 0:L, sl], v_ref[0, 0:L, sl]
                kw = k_ref[0, pl.ds(start, NA_KH * GRID_W), sl]
                vw = v_ref[0, pl.ds(start, NA_KH * GRID_W), sl]
                q2 = heads_stack(q_ref[0, qrows, sl])
                s_lat = lax.dot_general(q2, kw, NT_DIMS, preferred_element_type=F32) + bias_ref[jj, r - rs]
                s_ctx = lax.dot_general(q2, kc, NT_DIMS, preferred_element_type=F32)
                m = jnp.maximum(jnp.max(s_lat, axis=-1, keepdims=True), jnp.max(s_ctx, axis=-1, keepdims=True))
                p_lat = jnp.exp2(s_lat - m)
                p_ctx = jnp.exp2(s_ctx - m)
                oa = (jnp.dot(p_lat.astype(BF16), jnp.concatenate([vw, ones_w], axis=1), preferred_element_type=F32)
                      + jnp.dot(p_ctx.astype(BF16), jnp.concatenate([vc, ones_c], axis=1), preferred_element_type=F32))
                o = oa[:, :LANES] / oa[:, LANES:LANES + 1]
                o_ref[0, qrows, sl] = heads_merge(o).astype(BF16)


NA_PAIRS = 4


def na_attention(q, k, v, bias):
    B, Tt, _ = q.shape
    L = CTX_LEN
    rows = (Tt - L) // GRID_W
    w = NA_PAIRS * LANES
    return pl.pallas_call(
        functools.partial(_na_kernel, L=L, rows=rows),
        grid=(B, NA_W // w, Tt // TM),
        in_specs=[pl.BlockSpec((1, TM, w), lambda b, j, t: (b, t, j)),
                  pl.BlockSpec((1, Tt, w), lambda b, j, t: (b, 0, j)),
                  pl.BlockSpec((1, Tt, w), lambda b, j, t: (b, 0, j)),
                  pl.BlockSpec((NA_PAIRS,) + bias.shape[1:], lambda b, j, t: (j, 0, 0, 0),
                               pipeline_mode=pl.Buffered(1))],
        out_specs=pl.BlockSpec((1, TM, w), lambda b, j, t: (b, t, j)),
        out_shape=jax.ShapeDtypeStruct((B, Tt, NA_W), BF16),
        compiler_params=_cparams(3), name="na_attention",
    )(q, k, v, bias)


def na_bias_tables(rpb):
    depth, H = rpb.shape[:2]
    cq = jnp.arange(GRID_W)
    col_start = jnp.clip(cq - NA_KW // 2, 0, GRID_W - NA_KW)
    col_mask = (cq[None, :] >= col_start[:, None]) & (cq[None, :] < col_start[:, None] + NA_KW)
    col_idx = jnp.clip(cq[None, :] - cq[:, None] + NA_KW - 1, 0, 2 * NA_KW - 2)
    pick = (col_idx[None] == jnp.arange(2 * NA_KW - 1)[:, None, None]).astype(F32)
    cols = jnp.einsum('lhrc,cqk->lhrqk', rpb, pick, precision=lax.Precision.HIGHEST)
    cols = jnp.where(col_mask, cols, NEG)
    tbl = jnp.stack([cols[:, :, NA_KH - 1 - dl:2 * NA_KH - 1 - dl] for dl in range(NA_KH)], axis=2)
    tbl = jnp.transpose(tbl, (0, 1, 2, 4, 3, 5)).reshape(depth, H // 2, 2, NA_KH, GRID_W, NA_KH * GRID_W)
    return jnp.transpose(tbl, (0, 1, 3, 2, 4, 5)).reshape(depth, H // 2, NA_KH, 2 * GRID_W, NA_KH * GRID_W)


def _log_sigmoid(x):
    return jnp.minimum(x, 0.0) - jnp.log(1.0 + jnp.exp(-jnp.abs(x)))


def _mlstm_gate_kernel(gf_ref, gb_ref, rqf_ref, rqb_ref, col_ref, t_scr):
    nc, n = gf_ref.shape[1], M_CH
    rows = nc * 8
    lane = lax.broadcasted_iota(jnp.int32, (rows, n), 1)
    for d, (g_ref, rq_ref) in enumerate(((gf_ref, rqf_ref), (gb_ref, rqb_ref))):
        def scan(x, op, fill):
            sh = 1
            while sh < n:
                if d == 0:
                    x = op(x, jnp.where(lane >= sh, pltpu.roll(x, sh, 1), fill))
                else:
                    x = op(x, jnp.where(lane < n - sh, pltpu.roll(x, n - sh, 1), fill))
                sh *= 2
            return x

        top = g_ref[0, :, 0:8, :].reshape(rows, n)
        bot = g_ref[0, :, 8:16, :].reshape(rows, n)
        b = scan(_log_sigmoid(bot), jnp.add, 0.0)
        last = n - 1 if d == 0 else 0
        tot = jnp.broadcast_to(b[:, last:last + 1], (rows, n))
        r = top - b
        cm = scan(r, jnp.maximum, NEG)
        amax = jnp.broadcast_to(jnp.max(tot + r, axis=1, keepdims=True), (rows, n))
        for j, val in enumerate((r, cm, tot, amax)):
            rq_ref[0, :, 8 * j:8 * (j + 1), :] = val.reshape(nc, 8, n)
        t_scr[:, 16 * d:16 * d + 8, :] = b.reshape(nc, 8, n)
        t_scr[:, 16 * d + 8:16 * d + 16, :] = cm.reshape(nc, 8, n)

    def transpose_chunk(ci, carry):
        tile = jnp.concatenate([t_scr[ci], jnp.zeros((n - 32, n), F32)], axis=0)
        col_ref[0, ci] = tile.T
        return carry

    lax.fori_loop(0, nc, transpose_chunk, 0)


def mlstm_gates(gf, gb):
    B, nc, _, n = gf.shape
    blk = lambda r: pl.BlockSpec((1, nc, r, n), lambda b: (b, 0, 0, 0))
    return pl.pallas_call(
        _mlstm_gate_kernel,
        grid=(B,),
        in_specs=[blk(16), blk(16)],
        out_specs=[blk(32), blk(32), blk(n)],
        out_shape=[jax.ShapeDtypeStruct((B, nc, 32, n), F32)] * 2 + [jax.ShapeDtypeStruct((B, nc, n, n), F32)],
        scratch_shapes=[pltpu.VMEM((nc, 32, n), F32)],
        compiler_params=_cparams(1), name="mlstm_gates",
    )(gf, gb)


def _mlstm_chain_group(d, q_ref, k_ref, v_ref, rq_ref, col_ref, o_ref, ct_ref, m_ref, bb):
    n, H = M_CH, M_HEADS
    row8 = lax.broadcasted_iota(jnp.int32, (2 * H, n), 0)
    lane = lax.broadcasted_iota(jnp.int32, (n, n), 1)
    sub = lax.broadcasted_iota(jnp.int32, (n, n), 0)
    eye = lane == sub
    causal = (lane <= sub) if d == 0 else (lane >= sub)
    one0 = jnp.ones((n, LANES), BF16)

    r8, cm8 = rq_ref[bb, 0, 0:8, :], rq_ref[bb, 0, 8:16, :]
    tot, amax = rq_ref[bb, 0, 16:24, :], rq_ref[bb, 0, 24:32, :]
    m0 = m_ref[d, bb]
    nm8 = jnp.maximum(m0, cm8)
    m_new = jnp.maximum(tot + m0, amax)
    decay8 = jnp.exp(tot + m0 - m_new)
    w8 = jnp.exp(tot + r8 - m_new)
    e8 = jnp.exp(m0 - nm8)
    m_ref[d, bb] = jnp.where(row8 < H, m_new, 0.0)
    tc = col_ref[bb, 0]

    for hh in range(H):
        sl = slice(hh * M_HEAD_DIM, (hh + 1) * M_HEAD_DIM)
        qh, kh, vh = q_ref[bb, :, sl], k_ref[bb, :, sl], v_ref[bb, :, sl]
        b_col = tc[:, 16 * d + hh:16 * d + hh + 1]
        nm_col = jnp.maximum(m0[hh:hh + 1, 0:1], tc[:, 16 * d + 8 + hh:16 * d + 8 + hh + 1])
        dmat = jnp.where(causal, jnp.exp(r8[hh:hh + 1, :] - nm_col), 0.0)
        s = (lax.dot_general(qh, kh, NT_DIMS, preferred_element_type=F32) * dmat).astype(BF16)
        e_diag = jnp.where(eye, e8[hh:hh + 1, :], 0.0).astype(BF16)
        w_diag = jnp.where(eye, w8[hh:hh + 1, :], 0.0).astype(BF16)
        vaug = jnp.concatenate([vh, one0], axis=1)
        ct = ct_ref[d, bb, hh]
        inter = jnp.dot(qh, ct.astype(BF16), preferred_element_type=F32).astype(BF16)
        num = jnp.dot(jnp.concatenate([s, e_diag], axis=1), jnp.concatenate([vaug, inter], axis=0),
                      preferred_element_type=F32)
        den = num[:, M_HEAD_DIM:M_HEAD_DIM + LANES]
        o_ref[bb, :, sl] = num[:, :M_HEAD_DIM] / jnp.maximum(jnp.abs(den), jnp.exp(-(b_col + nm_col)))
        wv = jnp.dot(w_diag, vaug, preferred_element_type=F32).astype(BF16)
        ct_ref[d, bb, hh] = decay8[hh:hh + 1, 0:1] * ct + lax.dot_general(kh, wv, TN_DIMS, preferred_element_type=F32)


def _mlstm_kernel(qf_ref, kf_ref, vf_ref, rqf_ref, colf_ref, qb_ref, kb_ref, vb_ref, rqb_ref, colb_ref,
                  of_ref, ob_ref, ct_ref, m_ref):
    @pl.when(pl.program_id(1) == 0)
    def _():
        ct_ref[...] = jnp.zeros_like(ct_ref)
        m_ref[...] = jnp.zeros_like(m_ref)

    for bb in range(qf_ref.shape[0]):
        _mlstm_chain_group(0, qf_ref, kf_ref, vf_ref, rqf_ref, colf_ref, of_ref, ct_ref, m_ref, bb)
        _mlstm_chain_group(1, qb_ref, kb_ref, vb_ref, rqb_ref, colb_ref, ob_ref, ct_ref, m_ref, bb)


M_BATCH = 4


def mlstm(q, k, v, gf, gb):
    B, Tt, _ = q.shape
    nc = Tt // M_CH
    lc = CTX_LEN // M_CH
    nb = M_BATCH if B % M_BATCH == 0 else 1
    rqf, rqb, col = mlstm_gates(gf, gb)

    def bw(s):
        return jnp.where(s < lc, lc - 1 - s, nc - 1 + lc - s)

    tok_f = pl.BlockSpec((nb, M_CH, M_W), lambda b, s: (b, s, 0))
    tok_b = pl.BlockSpec((nb, M_CH, M_W), lambda b, s: (b, bw(s), 0))
    chunk_f = lambda r: pl.BlockSpec((nb, 1, r, M_CH), lambda b, s: (b, s, 0, 0))
    chunk_b = lambda r: pl.BlockSpec((nb, 1, r, M_CH), lambda b, s: (b, bw(s), 0, 0))
    return pl.pallas_call(
        _mlstm_kernel,
        grid=(B // nb, nc),
        in_specs=[tok_f, tok_f, tok_f, chunk_f(32), chunk_f(M_CH), tok_b, tok_b, tok_b, chunk_b(32), chunk_b(M_CH)],
        out_specs=[tok_f, tok_b],
        out_shape=[jax.ShapeDtypeStruct((B, Tt, M_W), F32)] * 2,
        scratch_shapes=[pltpu.VMEM((2, nb, M_HEADS, M_HEAD_DIM, M_HEAD_DIM + LANES), F32),
                        pltpu.VMEM((2, nb, 2 * M_HEADS, M_CH), F32)],
        compiler_params=_cparams(2), name="mlstm",
    )(q, k, v, rqf, col, q, k, v, rqb, col)


def _conv_tile(t, nt, cur_ref, prev_ref, next_ref, w_ref, b_ref, g_ref, be_ref, o_ref, hbuf, hsh):
    C = CONV_CH

    def glu(u):
        u = u.astype(F32)
        return u[:, :C] * _sigmoid(u[:, C:])

    prev_ok = t >= 2
    next_ok = jnp.logical_and(t >= 1, t < nt - 1)
    hbuf[0:CONV_HALO] = jnp.where(prev_ok, glu(prev_ref[0]), 0.0)
    hbuf[CONV_HALO:CONV_HALO + TM] = glu(cur_ref[0])
    hbuf[CONV_HALO + TM:2 * CONV_HALO + TM] = jnp.where(next_ok, glu(next_ref[0]), 0.0)
    span = TM + 2 * CONV_HALO - 8
    for r in range(8):
        hsh[r] = hbuf[r:r + span, :]
    off = CONV_HALO - CONV_K // 2
    for rc in range(TM // CONV_ROWS):
        acc = jnp.zeros((CONV_ROWS, C), F32) + b_ref[...]
        for j in range(CONV_K):
            a, r = divmod(off + j, 8)
            acc = acc + hsh[r, rc * CONV_ROWS + 8 * a:rc * CONV_ROWS + 8 * a + CONV_ROWS, :] * w_ref[j:j + 1, :]
        mu = jnp.mean(acc, axis=-1, keepdims=True)
        xc = acc - mu
        var = jnp.mean(xc * xc, axis=-1, keepdims=True)
        y = xc * lax.rsqrt(var + NORM_EPS) * g_ref[...] + be_ref[...]
        o_ref[rc * CONV_ROWS:(rc + 1) * CONV_ROWS, :] = (y * _sigmoid(y)).astype(BF16)


def _merge_kernel(na_ref, hfw_ref, hbw_ref, mo_ref, cu_ref, cup_ref, cun_ref, bg_ref, x_ref, mod_ref, ng_ref,
                  cw_ref, cb_ref, cg_ref, cbe_ref, wna_ref, wm_ref, wc_ref, wo_ref, wr_ref,
                  x1_ref, h2_ref, aff_ref, cv_ref, hbuf, hsh, *, D, nt):
    _conv_tile(pl.program_id(1), nt, cu_ref, cup_ref, cun_ref, cw_ref, cb_ref, cg_ref, cbe_ref, cv_ref, hbuf, hsh)
    mod = mod_ref[0, 0]
    mh = hfw_ref[0] + hbw_ref[0]
    parts = []
    for hh in range(M_HEADS):
        xh = mh[:, hh * M_HEAD_DIM:(hh + 1) * M_HEAD_DIM]
        parts.append(xh * lax.rsqrt(jnp.mean(xh * xh, axis=-1, keepdims=True) + NORM_EPS))
    mn = (jnp.concatenate(parts, axis=1) * ng_ref[...] * _sigmoid(mo_ref[0].astype(F32))).astype(BF16)
    z = (_sigmoid(bg_ref[0, :, 0:D].astype(F32)) * jnp.dot(na_ref[0], wna_ref[...], preferred_element_type=F32)
         + _sigmoid(bg_ref[0, :, D:2 * D].astype(F32)) * jnp.dot(mn, wm_ref[...], preferred_element_type=F32)
         + _sigmoid(bg_ref[0, :, 2 * D:3 * D].astype(F32)) * jnp.dot(cv_ref[...], wc_ref[...], preferred_element_type=F32))
    y = jnp.dot(z.astype(BF16), wo_ref[...], preferred_element_type=F32)
    x1 = x_ref[0] + mod[:, 2 * D:3 * D] * y
    x1_ref[0] = x1
    h2 = _modnorm(x1, mod[:, 3 * D:4 * D], mod[:, 4 * D:5 * D]).astype(BF16)
    h2_ref[0] = h2
    logits = lax.dot_general(wr_ref[...], h2, NT_DIMS, preferred_element_type=F32)
    e = jnp.exp(logits - jnp.max(logits, axis=0, keepdims=True))
    aff_ref[0] = e / jnp.sum(e, axis=0, keepdims=True)


def merge_and_route(na, hfw, hbw, mo, cu, bg, X, modsel, norm_gain, conv_w, conv_b, ln_g, ln_b,
                    w_na, w_m, w_c, w_o, w_rt):
    B, Tt, D = X.shape
    E = w_rt.shape[0]
    C = CONV_CH
    nt = Tt // TM
    hb = TM // CONV_HALO
    nhb = Tt // CONV_HALO
    wpad = jnp.zeros((32, C), F32).at[:CONV_K].set(conv_w)
    vec = lambda a: a.reshape(1, C)
    tok = lambda w: pl.BlockSpec((1, TM, w), lambda b, t: (b, t, 0))
    const = lambda shape: pl.BlockSpec(shape, lambda b, t: (0,) * len(shape))
    return pl.pallas_call(
        functools.partial(_merge_kernel, D=D, nt=nt),
        grid=(B, nt),
        in_specs=[tok(NA_W), tok(M_W), tok(M_W), tok(M_W), tok(2 * C),
                  pl.BlockSpec((1, CONV_HALO, 2 * C), lambda b, t: (b, jnp.maximum(t * hb - 1, 0), 0)),
                  pl.BlockSpec((1, CONV_HALO, 2 * C), lambda b, t: (b, jnp.minimum((t + 1) * hb, nhb - 1), 0)),
                  tok(3 * D), tok(D),
                  pl.BlockSpec((1, 1, 1, 6 * D), lambda b, t: (b, jnp.minimum(t, 1), 0, 0)),
                  const((1, M_W)), const((32, C)), const((1, C)), const((1, C)), const((1, C)),
                  const((NA_W, D)), const((M_W, D)), const((C, D)), const((D, D)), const((E, D))],
        out_specs=[tok(D), tok(D), pl.BlockSpec((1, E, TM), lambda b, t: (b, 0, t))],
        out_shape=[jax.ShapeDtypeStruct((B, Tt, D), F32), jax.ShapeDtypeStruct((B, Tt, D), BF16),
                   jax.ShapeDtypeStruct((B, E, Tt), F32)],
        scratch_shapes=[pltpu.VMEM((TM, C), BF16), pltpu.VMEM((TM + 2 * CONV_HALO, C), F32),
                        pltpu.VMEM((8, TM + 2 * CONV_HALO - 8, C), F32)],
        compiler_params=_cparams(2), name="merge_and_route",
    )(na, hfw, hbw, mo, cu, cu, cu, bg, X, modsel, norm_gain, wpad, vec(conv_b), vec(ln_g), vec(ln_b),
      w_na, w_m, w_c, w_o, w_rt)


def _topk_kernel(aff_ref, pos_ref, cnt_ref, *, L, k_ctx, k_lat):
    E, Tt = aff_ref.shape[1:]
    u = pltpu.bitcast(aff_ref[0], jnp.int32)
    is_ctx = lax.broadcasted_iota(jnp.int32, (E, Tt), 1) < L

    def count(mask):
        f = jnp.where(mask, 1.0, 0.0)
        c_ctx = jnp.sum(jnp.where(is_ctx, f, 0.0), axis=1, keepdims=True)
        return c_ctx, jnp.sum(f, axis=1, keepdims=True) - c_ctx

    def bit_step(i, carry):
        v_ctx, v_lat = carry
        bit = jnp.left_shift(jnp.int32(1), 30 - i)
        c_ctx, c_lat = count(u >= jnp.where(is_ctx, v_ctx | bit, v_lat | bit))
        return (jnp.where(c_ctx >= k_ctx, v_ctx | bit, v_ctx), jnp.where(c_lat >= k_lat, v_lat | bit, v_lat))

    zero = jnp.zeros((E, 1), jnp.int32)
    v_ctx, v_lat = lax.fori_loop(0, 31, bit_step, (zero, zero))
    thr = jnp.where(is_ctx, v_ctx, v_lat)
    g_ctx, g_lat = count(u > thr)
    need_ctx, need_lat = k_ctx - g_ctx, k_lat - g_lat

    r = lax.broadcasted_iota(jnp.int32, (TM, TM), 0)
    c = lax.broadcasted_iota(jnp.int32, (TM, TM), 1)
    tri = jnp.where(r < c, 1.0, 0.0).astype(BF16)
    lane_t = lax.broadcasted_iota(jnp.int32, (E, LANES), 1)
    cnt = jnp.zeros((E, LANES), F32)
    ties = jnp.zeros((E, 1), F32)
    slots = jnp.zeros((E, 1), F32)
    nt = Tt // TM
    for t in range(nt):
        if t == 1:
            ties = jnp.zeros((E, 1), F32)
        need = need_ctx if t == 0 else need_lat
        ut = pltpu.bitcast(aff_ref[0, :, t * TM:(t + 1) * TM], jnp.int32)
        tt = v_ctx if t == 0 else v_lat
        eq = jnp.where(ut == tt, 1.0, 0.0)
        rank = ties + jnp.dot(eq.astype(BF16), tri, preferred_element_type=F32)
        sel = jnp.where(ut > tt, 1.0, jnp.where(rank < need, eq, 0.0))
        pos = slots + jnp.dot(sel.astype(BF16), tri, preferred_element_type=F32)
        pos_ref[0, :, t * TM:(t + 1) * TM] = jnp.where(sel > 0.0, pos, -1.0).astype(jnp.int32)
        cnt = jnp.where(lane_t == t, slots, cnt)
        ties = ties + jnp.sum(eq, axis=1, keepdims=True)
        slots = slots + jnp.sum(sel, axis=1, keepdims=True)
    cnt = jnp.where(lane_t == nt, slots, cnt)
    cnt_ref[0] = cnt.astype(jnp.int32)


def expert_choice_slots(aff):
    B, E, Tt = aff.shape
    L = CTX_LEN
    k_ctx = EC_FACTOR * L // N_EXPERTS
    k_lat = EC_FACTOR * (Tt - L) // N_EXPERTS
    assert Tt // TM + 1 <= LANES
    return pl.pallas_call(
        functools.partial(_topk_kernel, L=L, k_ctx=k_ctx, k_lat=k_lat),
        grid=(B,),
        in_specs=[pl.BlockSpec((1, E, Tt), lambda b: (b, 0, 0))],
        out_specs=[pl.BlockSpec((1, E, Tt), lambda b: (b, 0, 0)), pl.BlockSpec((1, E, LANES), lambda b: (b, 0, 0))],
        out_shape=[jax.ShapeDtypeStruct((B, E, Tt), jnp.int32), jax.ShapeDtypeStruct((B, E, LANES), jnp.int32)],
        compiler_params=_cparams(1), name="expert_choice_slots",
    )(aff)


def _ffn_kernel(cnt_ref, h_ref, pos_ref, wg_ref, wu_ref, wd_ref, y_ref, xin_ref, acc_ref, *, ns, nt, unroll):
    b, e, f = pl.program_id(0), pl.program_id(1), pl.program_id(2)
    W = SLOT_WIN
    G = xin_ref.shape[0]
    eg = lax.rem(e, G)

    @pl.when(f == 0)
    def _():
        acc_ref[...] = jnp.zeros_like(acc_ref)

    @pl.when(jnp.logical_and(f == 0, eg == 0))
    def _gather():
        xin_ref[...] = jnp.zeros_like(xin_ref)

        def onehot(q, t, base):
            srow = base + lax.broadcasted_iota(jnp.int32, (W, TM), 0)
            return jnp.where(pos_ref[0, q, :, pl.ds(pl.multiple_of(t * TM, TM), TM)] == srow, 1.0, 0.0).astype(BF16)

        def rows_of(t, onehots):
            return jnp.dot(onehots, h_ref[0, pl.ds(pl.multiple_of(t * TM, TM), TM), :], preferred_element_type=F32)

        def add_rows(q, base, rows):
            xin_ref[q, pl.ds(base, W), :] = xin_ref[q, pl.ds(base, W), :] + rows.astype(BF16)

        def group_body(g, carry):
            tiles = [g * unroll + k for k in range(unroll)]
            bases = [[pl.multiple_of((cnt_ref[b, e + q, t] // 16) * 16, 16) for q in range(G)] for t in tiles]
            first = [rows_of(t, jnp.concatenate([onehot(q, t, bs[q]) for q in range(G)], axis=0))
                     for t, bs in zip(tiles, bases)]
            for t, bs, rows in zip(tiles, bases, first):
                for q in range(G):
                    add_rows(q, bs[q], rows[q * W:(q + 1) * W])

                    def more(r, c2, t=t, q=q, base=bs[q]):
                        nxt = pl.multiple_of(base + r * W, 16)
                        add_rows(q, nxt, rows_of(t, onehot(q, t, nxt)))
                        return c2

                    lax.fori_loop(1, (cnt_ref[b, e + q, t + 1] - bs[q] + W - 1) // W, more, 0)
            return carry

        lax.fori_loop(0, nt // unroll, group_body, 0)

    x = xin_ref[eg, 0:ns, :]
    a = jnp.dot(x, wg_ref[0, 0].astype(BF16), preferred_element_type=F32)
    u = jnp.dot(x, wu_ref[0, 0].astype(BF16), preferred_element_type=F32)
    hmid = (a * _sigmoid(a) * u).astype(BF16)
    acc_ref[...] += jnp.dot(hmid, wd_ref[0, 0].astype(BF16), preferred_element_type=F32)

    @pl.when(f == pl.num_programs(2) - 1)
    def _():
        y_ref[0, 0, 0:ns, :] = acc_ref[...].astype(BF16)
        y_ref[0, 0, ns:ns + SLOT_WIN, :] = jnp.zeros((SLOT_WIN, y_ref.shape[-1]), BF16)


FFN_GATHER_EXPERTS = 4


def expert_ffn(cnt, h2, pos, w_gate, w_up, w_down, layer):
    B, Tt, D = h2.shape
    _, E, _, F = w_gate.shape
    L = CTX_LEN
    ns = EC_FACTOR * L // N_EXPERTS + EC_FACTOR * (Tt - L) // N_EXPERTS
    assert ns % 16 == 0
    tf = min(512, F)
    nt = Tt // TM
    unroll = max(u for u in range(1, 4) if nt % u == 0)
    G = FFN_GATHER_EXPERTS
    assert E % G == 0
    grid_spec = pltpu.PrefetchScalarGridSpec(
        num_scalar_prefetch=1,
        grid=(B, E, F // tf),
        in_specs=[pl.BlockSpec((1, Tt, D), lambda b, e, f, c: (b, 0, 0), pipeline_mode=pl.Buffered(1)),
                  pl.BlockSpec((1, G, 1, Tt), lambda b, e, f, c: (b, e // G, 0, 0)),
                  pl.BlockSpec((1, 1, D, tf), lambda b, e, f, c: (layer, e, 0, f)),
                  pl.BlockSpec((1, 1, D, tf), lambda b, e, f, c: (layer, e, 0, f)),
                  pl.BlockSpec((1, 1, tf, D), lambda b, e, f, c: (layer, e, f, 0))],
        out_specs=pl.BlockSpec((1, 1, ns + SLOT_WIN, D), lambda b, e, f, c: (b, e, 0, 0)),
        scratch_shapes=[pltpu.VMEM((G, ns + SLOT_WIN, D), BF16), pltpu.VMEM((ns, D), F32)])
    return pl.pallas_call(
        functools.partial(_ffn_kernel, ns=ns, nt=nt, unroll=unroll),
        grid_spec=grid_spec,
        out_shape=jax.ShapeDtypeStruct((B, E, ns + SLOT_WIN, D), BF16),
        compiler_params=_cparams(3), name="expert_ffn",
    )(cnt, h2, pos.reshape(B, E, 1, Tt), w_gate, w_up, w_down)


def _combine_kernel(cnt_ref, pos_ref, aff_ref, y_ref, x_ref, mod_ref, o_ref, yw_ref, p_ref, acc_ref, *, D, ns, t0):
    b, t = pl.program_id(0), pl.program_id(1) + t0
    E = pos_ref.shape[1]
    W = SLOT_WIN
    starts, rounds = [], jnp.int32(0)
    for e in range(E):
        lo, hi = cnt_ref[b, e, t], cnt_ref[b, e, t + 1]
        s0 = (lo // 16) * 16
        starts.append(s0)
        rounds = jnp.maximum(rounds, jnp.where(hi > lo, (hi - s0 + W - 1) // W, 0))
    def round_sum(r):
        for e in range(E):
            base = pl.multiple_of(jnp.minimum(starts[e] + r * W, ns), 16)
            yw_ref[e * W:(e + 1) * W, :] = y_ref[0, e, pl.ds(base, W), :]
            srow = base + lax.broadcasted_iota(jnp.int32, (W, TM), 0)
            p_ref[e * W:(e + 1) * W, :] = jnp.where(pos_ref[0, e:e + 1, :] == srow, aff_ref[0, e:e + 1, :], 0.0).astype(BF16)
        return lax.dot_general(p_ref[...], yw_ref[...], TN_DIMS, preferred_element_type=F32)

    acc_ref[...] = round_sum(0)

    def extra_round(r, carry):
        acc_ref[...] += round_sum(r)
        return carry

    lax.fori_loop(1, rounds, extra_round, 0)
    o_ref[0] = x_ref[0] + mod_ref[0, 0][:, 5 * D:6 * D] * acc_ref[...]


def combine(cnt, pos, aff, Y, X1, modsel, latent_only):
    B, Tt, D = X1.shape
    E = pos.shape[1]
    nsp = Y.shape[2]
    t0 = 1 if latent_only else 0
    grid_spec = pltpu.PrefetchScalarGridSpec(
        num_scalar_prefetch=1,
        grid=(B, Tt // TM - t0),
        in_specs=[pl.BlockSpec((1, E, TM), lambda b, t, c: (b, 0, t + t0)),
                  pl.BlockSpec((1, E, TM), lambda b, t, c: (b, 0, t + t0)),
                  pl.BlockSpec((1, E, nsp, D), lambda b, t, c: (b, 0, 0, 0), pipeline_mode=pl.Buffered(1)),
                  pl.BlockSpec((1, TM, D), lambda b, t, c: (b, t + t0, 0)),
                  pl.BlockSpec((1, 1, 1, 6 * D), lambda b, t, c: (b, jnp.minimum(t + t0, 1), 0, 0))],
        out_specs=pl.BlockSpec((1, TM, D), lambda b, t, c: (b, t, 0)),
        scratch_shapes=[pltpu.VMEM((E * SLOT_WIN, D), BF16), pltpu.VMEM((E * SLOT_WIN, TM), BF16),
                        pltpu.VMEM((TM, D), F32)])
    return pl.pallas_call(
        functools.partial(_combine_kernel, D=D, ns=nsp - SLOT_WIN, t0=t0),
        grid_spec=grid_spec,
        out_shape=jax.ShapeDtypeStruct((B, Tt - t0 * TM, D), F32),
        compiler_params=_cparams(2), name="combine",
    )(cnt, pos, aff, Y, X1, modsel)


def _pack_w_in(w, D):
    widths = (NA_W, NA_W, NA_W, M_W, M_W, M_W, M_W, N_GATE_COLS, 2 * CONV_CH, 3 * D)
    offs = [0]
    for wd in widths:
        offs.append(offs[-1] + wd)
    seg = [w[:, offs[i]:offs[i + 1]] for i in range(len(widths))]
    return jnp.concatenate(seg[:7] + seg[8:], axis=1).astype(BF16), _gate_rows(seg[7].T).astype(BF16)


def _gate_rows(g):
    H = M_HEADS
    i_fw, f_fw, i_bw, f_bw = (g[j * H:(j + 1) * H] for j in range(4))
    return jnp.concatenate([i_fw, f_fw, f_fw, i_fw, i_bw, f_bw, f_bw, i_bw], axis=0)


def _rope_tables(T, L):
    nf = M_HEAD_DIM // 4
    t = jnp.arange(T)
    row = (t // GRID_W).astype(F32)
    col = (t % GRID_W).astype(F32)
    inv = ROPE_BASE ** (-jnp.arange(nf, dtype=F32) / nf)
    ar, ac = row[:, None] * inv[None, :], col[:, None] * inv[None, :]
    z = jnp.zeros((T, nf), F32)
    cos = jnp.concatenate([jnp.cos(ar), jnp.cos(ar), jnp.cos(ac), jnp.cos(ac)], axis=1)
    sina = jnp.concatenate([-jnp.sin(ar), z, -jnp.sin(ac), z], axis=1)
    sinb = jnp.concatenate([z, jnp.sin(ar), z, jnp.sin(ac)], axis=1)
    ident = lambda v: jnp.full((L, M_HEAD_DIM), v, F32)
    return (jnp.concatenate([ident(1.0), cos]), jnp.concatenate([ident(0.0), sina]),
            jnp.concatenate([ident(0.0), sinb]))


def kernel(x, c, ctx, c_ctx, w_mod, b_mod, w_in, na_q_gain, na_k_gain, na_rpb, m_gate_bias,
           m_norm_gain, conv_w, conv_b, conv_ln_g, conv_ln_b, w_br_na, w_br_m, w_br_conv, w_out,
           w_router, w_gate, w_up, w_down):
    B, T, D = x.shape
    L = ctx.shape[1]
    assert L == CTX_LEN == TM and T % TM == 0 and B + 1 <= 16
    depth = w_in.shape[0]
    X = jnp.concatenate([ctx, x], axis=1)
    cc = jnp.zeros((16, D), F32).at[:B].set(c).at[B].set(c_ctx)
    mods = modulation(cc, w_mod, b_mod)
    cos, sina, sinb = _rope_tables(T, L)
    hd = jnp.arange(NA_W) // NA_HEAD_DIM
    gmat = (hd[:, None] == hd[None, :]).astype(BF16)
    bias_tables = na_bias_tables(na_rpb * LOG2E)
    for l in range(depth):
        modsel = jnp.stack([jnp.broadcast_to(mods[l, B], (B, 6 * D)), mods[l, :B]], axis=1).reshape(B, 2, 1, 6 * D)
        q_gain = (jnp.tile(na_q_gain[l], NA_HEADS) * (NA_HEAD_DIM ** -0.5 * LOG2E)).reshape(1, NA_W)
        k_gain = jnp.tile(na_k_gain[l], NA_HEADS).reshape(1, NA_W)
        w_tok, w_gates_t = _pack_w_in(w_in[l], D)
        naq, nak, nav, mq, mk, mv, mo, mgf, mgb, cu, bg = input_projection(
            X, modsel, w_tok, q_gain, k_gain, gmat, cos, sina, sinb, w_gates_t,
            _gate_rows(m_gate_bias[l].reshape(N_GATE_COLS, 1)))
        na = na_attention(naq, nak, nav, bias_tables[l])
        hfw, hbw = mlstm(mq, mk, mv, mgf, mgb)
        X1, h2, aff = merge_and_route(
            na, hfw, hbw, mo, cu, bg, X, modsel, m_norm_gain[l].reshape(1, M_W),
            conv_w[l], conv_b[l], conv_ln_g[l], conv_ln_b[l],
            w_br_na[l].astype(BF16), w_br_m[l].astype(BF16), w_br_conv[l].astype(BF16), w_out[l].astype(BF16),
            w_router[l].T.astype(BF16))
        pos, cnt = expert_choice_slots(aff)
        cnt = cnt[:, :, :T // TM + 2]
        Y = expert_ffn(cnt, h2, pos, w_gate, w_up, w_down, l)
        X = combine(cnt, pos, aff, Y, X1, modsel, latent_only=(l == depth - 1))
    return X
```
